```python
import math
import jax, jax.numpy as jnp
from jax import lax
import numpy as np

D_MODEL = 1024
BATCH = 32
SEQ = 2048
DEPTH = 4

CHUNK = 64
N_EVEN = (DEPTH + 1) // 2
N_ODD = DEPTH // 2
D_FF = 2816
DEEPNORM_ALPHA = (2 * DEPTH) ** 0.25
DEEPNORM_BETA = (8 * DEPTH) ** -0.25
LN_EPS = 1e-5
NEG_INF = -1e30
ATT_HEAD_DIM = 64
ATT_WIDTH = 3 * D_MODEL // 4
ATT_HEADS = ATT_WIDTH // ATT_HEAD_DIM
N_PREV_CHUNKS = 8
N_BAND = N_PREV_CHUNKS + 1
REL_CLIP = 128
SSM_WIDTH = D_MODEL - ATT_WIDTH
SSM_GROUP = 16
SSM_GROUPS = SSM_WIDTH // SSM_GROUP
SSM_STATE = 64
DT_MIN = 1e-3
DT_MAX = 1e-1
POOL_WINDOWS = (2, 4, 8, 16)
POOL_WIDTH = D_MODEL // 2
POOL_GROUP = POOL_WIDTH // len(POOL_WINDOWS)
SGU_WIDTH = D_MODEL - POOL_WIDTH
SGU_CHUNK = 128
SGU_HEADS = 4
SGU_HEAD_DIM = SGU_WIDTH // SGU_HEADS
AB_IN = 3 * ATT_WIDTH + SSM_WIDTH
CD_IN = POOL_WIDTH + 2 * SGU_WIDTH

kernel_name = 'hybrid_chunk_causal_encoder'


def layer_norm(x, g, b):
    xf = x.astype(jnp.float32)
    mu = jnp.mean(xf, axis=-1, keepdims=True)
    var = jnp.mean(jnp.square(xf - mu), axis=-1, keepdims=True)
    y = (xf - mu) * lax.rsqrt(var + LN_EPS)
    return (y * g.astype(jnp.float32) + b.astype(jnp.float32)).astype(x.dtype)


def swiglu(x, w_gate, w_up, w_down):
    return (jax.nn.silu(x @ w_gate) * (x @ w_up)) @ w_down


def chunk_attention(q, k, v, rel_bias):
    bsz, seq, heads, dh = q.shape
    nc = seq // CHUNK
    q = q.reshape(bsz, nc, CHUNK, heads, dh)
    pad = ((0, 0), (N_PREV_CHUNKS, 0), (0, 0), (0, 0), (0, 0))
    kp = jnp.pad(k.reshape(bsz, nc, CHUNK, heads, dh), pad)
    vp = jnp.pad(v.reshape(bsz, nc, CHUNK, heads, dh), pad)
    scores = jnp.concatenate(
        [jnp.einsum('bnqhd,bnkhd->bnhqk', q, kp[:, j:j + nc],
                    preferred_element_type=jnp.float32) for j in range(N_BAND)],
        axis=-1)
    qi = np.arange(CHUNK)[:, None]
    km = np.arange(N_BAND * CHUNK)[None, :]
    rel = np.clip(N_PREV_CHUNKS * CHUNK + qi - km, -REL_CLIP, REL_CLIP) + REL_CLIP
    bias = rel_bias[:, rel].astype(jnp.float32)
    chunk_ok = (np.arange(nc)[:, None] - N_PREV_CHUNKS + np.arange(N_BAND)[None, :]) >= 0
    valid = jnp.asarray(np.repeat(chunk_ok, CHUNK, axis=1))
    scores = scores * (dh ** -0.5) + bias[None, None]
    scores = jnp.where(valid[None, :, None, None, :], scores, NEG_INF)
    p = jax.nn.softmax(scores, axis=-1).astype(v.dtype)
    out = jnp.einsum('bnhqk,bnkhd->bnqhd', p[..., :CHUNK], vp[:, 0:nc])
    for j in range(1, N_BAND):
        out = out + jnp.einsum('bnhqk,bnkhd->bnqhd',
                               p[..., j * CHUNK:(j + 1) * CHUNK], vp[:, j:j + nc])
    return out.reshape(bsz, seq, heads * dh)


def _complex_linear_combine(e1, e2):
    a1r, a1i, b1r, b1i = e1
    a2r, a2i, b2r, b2i = e2
    ar = a2r * a1r - a2i * a1i
    ai = a2r * a1i + a2i * a1r
    br = a2r * b1r - a2i * b1i + b2r
    bi = a2r * b1i + a2i * b1r + b2i
    return (ar, ai, br, bi)


def s5_mixer(u, a_re, a_im, log_dt, b_re, b_im, c_re, c_im, d, w_glu, b_glu):
    bsz, seq, width = u.shape
    uf = u.astype(jnp.float32)
    ug = uf.reshape(bsz, seq, SSM_GROUPS, SSM_GROUP)
    a_re = a_re.astype(jnp.float32)
    a_im = a_im.astype(jnp.float32)
    dt = jnp.exp(log_dt.astype(jnp.float32))[:, None]
    mag = jnp.exp(dt * a_re)
    ang = dt * a_im
    ab_re = mag * jnp.cos(ang)
    ab_im = mag * jnp.sin(ang)
    den = a_re * a_re + a_im * a_im
    nr = ab_re - 1.0
    ni = ab_im
    coef_re = (nr * a_re + ni * a_im) / den
    coef_im = (ni * a_re - nr * a_im) / den
    b_re = b_re.astype(jnp.float32)
    b_im = b_im.astype(jnp.float32)
    bb_re = coef_re[..., None] * b_re - coef_im[..., None] * b_im
    bb_im = coef_re[..., None] * b_im + coef_im[..., None] * b_re
    bu_re = jnp.einsum('blgh,gph->lbgp', ug, bb_re)
    bu_im = jnp.einsum('blgh,gph->lbgp', ug, bb_im)
    shape_a = (seq, 1, SSM_GROUPS, SSM_STATE)
    at_re = jnp.broadcast_to(ab_re[None, None], shape_a)
    at_im = jnp.broadcast_to(ab_im[None, None], shape_a)
    _, _, s_re, s_im = lax.associative_scan(
        _complex_linear_combine, (at_re, at_im, bu_re, bu_im), axis=0)
    y = (jnp.einsum('lbgp,ghp->blgh', s_re, c_re.astype(jnp.float32))
         - jnp.einsum('lbgp,ghp->blgh', s_im, c_im.astype(jnp.float32)))
    y = y.reshape(bsz, seq, width) + d.astype(jnp.float32) * uf
    g = jax.nn.gelu(y)
    out = g * jax.nn.sigmoid(g @ w_glu.astype(jnp.float32) + b_glu.astype(jnp.float32))
    return out.astype(u.dtype)


def pool_mixer(xc, pool_w, pool_scale):
    bsz, seq, width = xc.shape
    xf = xc.astype(jnp.float32)
    cs = jnp.concatenate([jnp.zeros((bsz, 1, width), jnp.float32),
                          jnp.cumsum(xf, axis=1)], axis=1)
    t = np.arange(seq)
    outs = []
    for g, w in enumerate(POOL_WINDOWS):
        sl = slice(g * POOL_GROUP, (g + 1) * POOL_GROUP)
        lo = np.maximum(t + 1 - w, 0)
        cnt = np.minimum(t + 1, w).astype(np.float32)
        mean = (cs[:, 1:, sl] - cs[:, lo, sl]) / jnp.asarray(cnt)[None, :, None]
        outs.append(mean - xf[..., sl])
    pooled = jnp.stack(outs, axis=2)
    y = jnp.einsum('blgc,gcd->blgd', pooled, pool_w.astype(jnp.float32))
    y = y.reshape(bsz, seq, width) * pool_scale.astype(jnp.float32)
    return y.astype(xc.dtype)


def sgu_mixer(z, ln_g, ln_b, w_s, b_s):
    u, v = jnp.split(z, 2, axis=-1)
    v = layer_norm(v, ln_g, ln_b)
    bsz, seq, width = v.shape
    n = seq // SGU_CHUNK
    v = v.reshape(bsz, n, SGU_CHUNK, SGU_HEADS, SGU_HEAD_DIM)
    mask = jnp.tril(jnp.ones((SGU_CHUNK, SGU_CHUNK), dtype=w_s.dtype))
    sv = jnp.einsum('hts,bnshc->bnthc', w_s * mask[None], v) + b_s.T[None, None, :, :, None]
    return u * sv.reshape(bsz, seq, width)


def setup_inputs(seed: int = 0) -> dict:
    key = jax.random.key(seed)
    ks = iter(jax.random.split(key, 40))

    def nrm(shape, scale):
        return jax.random.normal(next(ks), shape, jnp.float32) * scale

    ne, no = N_EVEN, N_ODD
    x = nrm((BATCH, SEQ, D_MODEL), 1.0)
    ln_g = 1.0 + nrm((DEPTH, 3, D_MODEL), 0.02)
    ln_b = nrm((DEPTH, 3, D_MODEL), 0.02)
    ffn_w_gate = nrm((DEPTH, 2, D_MODEL, D_FF), D_MODEL ** -0.5)
    ffn_w_up = nrm((DEPTH, 2, D_MODEL, D_FF), D_MODEL ** -0.5)
    ffn_w_down = nrm((DEPTH, 2, D_FF, D_MODEL), D_FF ** -0.5 * DEEPNORM_BETA)
    ab_w_in = nrm((ne, D_MODEL, AB_IN), D_MODEL ** -0.5)
    ab_w_out = nrm((ne, D_MODEL, D_MODEL), D_MODEL ** -0.5 * DEEPNORM_BETA)
    att_rel_bias = nrm((ne, ATT_HEADS, 2 * REL_CLIP + 1), 0.5)
    ssm_shape = (ne, SSM_GROUPS, SSM_STATE)
    ssm_a_re = -0.5 + nrm(ssm_shape, 0.01)
    ssm_a_im = math.pi * jnp.arange(SSM_STATE, dtype=jnp.float32)[None, None, :] + nrm(ssm_shape, 0.01)
    ssm_log_dt = jax.random.uniform(next(ks), (ne, SSM_GROUPS), jnp.float32,
                                    minval=math.log(DT_MIN), maxval=math.log(DT_MAX))
    ssm_b_re = nrm((ne, SSM_GROUPS, SSM_STATE, SSM_GROUP), SSM_GROUP ** -0.5)
    ssm_b_im = nrm((ne, SSM_GROUPS, SSM_STATE, SSM_GROUP), SSM_GROUP ** -0.5)
    ssm_c_re = nrm((ne, SSM_GROUPS, SSM_GROUP, SSM_STATE), SSM_STATE ** -0.5)
    ssm_c_im = nrm((ne, SSM_GROUPS, SSM_GROUP, SSM_STATE), SSM_STATE ** -0.5)
    ssm_d = nrm((ne, SSM_WIDTH), 1.0)
    ssm_w_glu = nrm((ne, SSM_WIDTH, SSM_WIDTH), SSM_WIDTH ** -0.5)
    ssm_b_glu = nrm((ne, SSM_WIDTH), 0.01)
    cd_w_in = nrm((no, D_MODEL, CD_IN), D_MODEL ** -0.5)
    cd_w_out = nrm((no, D_MODEL, D_MODEL), D_MODEL ** -0.5 * DEEPNORM_BETA)
    pool_w = nrm((no, len(POOL_WINDOWS), POOL_GROUP, POOL_GROUP), POOL_GROUP ** -0.5)
    pool_scale = 1.0 + nrm((no, POOL_WIDTH), 0.02)
    sgu_ln_g = 1.0 + nrm((no, SGU_WIDTH), 0.02)
    sgu_ln_b = nrm((no, SGU_WIDTH), 0.02)
    sgu_w_s = nrm((no, SGU_HEADS, SGU_CHUNK, SGU_CHUNK), 0.5 * SGU_CHUNK ** -0.5)
    sgu_b_s = 1.0 + nrm((no, SGU_HEADS, SGU_CHUNK), 0.01)
    return {'x': x, 'ln_g': ln_g, 'ln_b': ln_b,
            'ffn_w_gate': ffn_w_gate, 'ffn_w_up': ffn_w_up, 'ffn_w_down': ffn_w_down,
            'ab_w_in': ab_w_in, 'ab_w_out': ab_w_out, 'att_rel_bias': att_rel_bias,
            'ssm_a_re': ssm_a_re, 'ssm_a_im': ssm_a_im, 'ssm_log_dt': ssm_log_dt,
            'ssm_b_re': ssm_b_re, 'ssm_b_im': ssm_b_im, 'ssm_c_re': ssm_c_re,
            'ssm_c_im': ssm_c_im, 'ssm_d': ssm_d, 'ssm_w_glu': ssm_w_glu,
            'ssm_b_glu': ssm_b_glu, 'cd_w_in': cd_w_in, 'cd_w_out': cd_w_out,
            'pool_w': pool_w, 'pool_scale': pool_scale, 'sgu_ln_g': sgu_ln_g,
            'sgu_ln_b': sgu_ln_b, 'sgu_w_s': sgu_w_s, 'sgu_b_s': sgu_b_s}


def reference(x, ln_g, ln_b, ffn_w_gate, ffn_w_up, ffn_w_down, ab_w_in, ab_w_out,
              att_rel_bias, ssm_a_re, ssm_a_im, ssm_log_dt, ssm_b_re, ssm_b_im,
              ssm_c_re, ssm_c_im, ssm_d, ssm_w_glu, ssm_b_glu, cd_w_in, cd_w_out,
              pool_w, pool_scale, sgu_ln_g, sgu_ln_b, sgu_w_s, sgu_b_s):
    bsz, seq, _ = x.shape
    for l in range(DEPTH):
        x = layer_norm(DEEPNORM_ALPHA * x + 0.5 * swiglu(x, ffn_w_gate[l, 0], ffn_w_up[l, 0],
                                                         ffn_w_down[l, 0]),
                       ln_g[l, 0], ln_b[l, 0])
        i = l // 2
        if l % 2 == 0:
            h = x @ ab_w_in[i]
            hs = (bsz, seq, ATT_HEADS, ATT_HEAD_DIM)
            q = h[..., :ATT_WIDTH].reshape(hs)
            k = h[..., ATT_WIDTH:2 * ATT_WIDTH].reshape(hs)
            v = h[..., 2 * ATT_WIDTH:3 * ATT_WIDTH].reshape(hs)
            us = h[..., 3 * ATT_WIDTH:]
            ya = chunk_attention(q, k, v, att_rel_bias[i])
            yb = s5_mixer(us, ssm_a_re[i], ssm_a_im[i], ssm_log_dt[i], ssm_b_re[i],
                          ssm_b_im[i], ssm_c_re[i], ssm_c_im[i], ssm_d[i],
                          ssm_w_glu[i], ssm_b_glu[i])
            mix = jnp.concatenate([ya, yb], axis=-1) @ ab_w_out[i]
        else:
            h = x @ cd_w_in[i]
            yc = pool_mixer(h[..., :POOL_WIDTH], pool_w[i], pool_scale[i])
            yd = sgu_mixer(jax.nn.gelu(h[..., POOL_WIDTH:]), sgu_ln_g[i], sgu_ln_b[i],
                           sgu_w_s[i], sgu_b_s[i])
            mix = jnp.concatenate([yc, yd], axis=-1) @ cd_w_out[i]
        x = layer_norm(DEEPNORM_ALPHA * x + mix, ln_g[l, 1], ln_b[l, 1])
        x = layer_norm(DEEPNORM_ALPHA * x + 0.5 * swiglu(x, ffn_w_gate[l, 1], ffn_w_up[l, 1],
                                                         ffn_w_down[l, 1]),
                       ln_g[l, 2], ln_b[l, 2])
    return x
```

```python
import functools

import jax
import jax.numpy as jnp
from jax import lax
from jax.experimental import pallas as pl
from jax.experimental.pallas import tpu as pltpu

BF16 = jnp.bfloat16
F32 = jnp.float32

DEPTH = 4
D_MODEL = 1024
D_FF = 2816
ALPHA = (2 * DEPTH) ** 0.25
LN_EPS = 1e-5
NEG_INF = -1e30

CHUNK = 64
N_PREV_CHUNKS = 8
REL_CLIP = 128
ATT_HEAD_DIM = 64
ATT_WIDTH = 768
ATT_HEADS = 12
SSM_WIDTH = 256
SSM_GROUP = 16
SSM_GROUPS = 16
SSM_STATE = 64
POOL_WINDOWS = (2, 4, 8, 16)
POOL_WIDTH = 512
POOL_GROUP = 128
SGU_WIDTH = 512
SGU_CHUNK = 128
SGU_HEADS = 4

V7X_LANES = 128
V7X_MXU_DIM = 256
V7X_VMEM_LIMIT_BYTES = 60000 * 1024

ROW_TILE = 512
FF_CHUNK = V7X_MXU_DIM
Q_TILE = 4 * CHUNK
K_WINDOW = Q_TILE + N_PREV_CHUNKS * CHUNK
SSM_T = 16
SSM_SEQS = 4
POOL_HALO = 16


def _params(n_axes):
    return pltpu.CompilerParams(
        dimension_semantics=("arbitrary",) * n_axes,
        vmem_limit_bytes=V7X_VMEM_LIMIT_BYTES)


def _resident(shape):
    nd = len(shape)
    return pl.BlockSpec(shape, lambda *_: (0,) * nd, pipeline_mode=pl.Buffered(1))


def _layer_norm(z, g, b):
    mu = jnp.mean(z, axis=-1, keepdims=True)
    zc = z - mu
    var = jnp.mean(zc * zc, axis=-1, keepdims=True)
    return zc * lax.rsqrt(var + LN_EPS) * g + b


def _gelu_tanh(x):
    return 0.5 * x * (1.0 + jnp.tanh(0.7978845608028654 * (x + 0.044715 * (x * x * x))))


def _dot(a, b):
    return jnp.dot(a, b, preferred_element_type=F32)


def _ffn_kernel(x_ref, wg_ref, wu_ref, wd_ref, g_ref, b_ref, o_ref, h_scr):
    x = x_ref[...]
    xb = x.astype(BF16)
    for c in range(D_FF // FF_CHUNK):
        sl = slice(c * FF_CHUNK, (c + 1) * FF_CHUNK)
        gate = _dot(xb, wg_ref[:, sl])
        up = _dot(xb, wu_ref[:, sl])
        h_scr[:, sl] = (gate * jax.nn.sigmoid(gate) * up).astype(BF16)
    y = _dot(h_scr[...], wd_ref[...])
    o_ref[...] = _layer_norm(ALPHA * x + 0.5 * y, g_ref[...], b_ref[...])


def _ffn_ln(x, wg, wu, wd, g, b):
    n = x.shape[0]
    row = pl.BlockSpec((ROW_TILE, D_MODEL), lambda i: (i, 0))
    return pl.pallas_call(
        _ffn_kernel,
        grid=(n // ROW_TILE,),
        in_specs=[row, _resident((D_MODEL, D_FF)), _resident((D_MODEL, D_FF)),
                  _resident((D_FF, D_MODEL)), _resident((1, D_MODEL)), _resident((1, D_MODEL))],
        out_specs=row,
        out_shape=jax.ShapeDtypeStruct((n, D_MODEL), F32),
        scratch_shapes=[pltpu.VMEM((ROW_TILE, D_FF), BF16)],
        compiler_params=_params(1),
        name="ffn_ln",
    )(x, wg, wu, wd, g, b)


def _ab_in_kernel(x_ref, w_ref, qkv_ref, us_ref):
    h = _dot(x_ref[...].astype(BF16), w_ref[...])
    qkv_ref[...] = h[:, :3 * ATT_WIDTH].astype(BF16)
    us_ref[...] = h[:, 3 * ATT_WIDTH:]


def _ab_in(x, w):
    n = x.shape[0]
    width = 3 * ATT_WIDTH + SSM_WIDTH
    return pl.pallas_call(
        _ab_in_kernel,
        grid=(n // ROW_TILE,),
        in_specs=[pl.BlockSpec((ROW_TILE, D_MODEL), lambda i: (i, 0)), _resident((D_MODEL, width))],
        out_specs=[pl.BlockSpec((ROW_TILE, 3 * ATT_WIDTH), lambda i: (i, 0)),
                   pl.BlockSpec((ROW_TILE, SSM_WIDTH), lambda i: (i, 0))],
        out_shape=[jax.ShapeDtypeStruct((n, 3 * ATT_WIDTH), BF16),
                   jax.ShapeDtypeStruct((n, SSM_WIDTH), F32)],
        compiler_params=_params(1),
        name="ab_in",
    )(x, w)


def _attn_kernel(q_ref, k_ref, v_ref, bias_ref, o_ref):
    seq = q_ref.shape[0]
    lane = lax.broadcasted_iota(jnp.int32, (Q_TILE, V7X_LANES), 1)
    first_head = lane < ATT_HEAD_DIM
    for t in range(seq // Q_TILE):
        q0 = t * Q_TILE
        k0 = max(0, q0 + Q_TILE - K_WINDOW)
        width = q0 + Q_TILE - k0
        q = q_ref[q0:q0 + Q_TILE, :]
        k = k_ref[k0:k0 + width, :]
        v = v_ref[k0:k0 + width, :]
        outs = []
        for hh in range(2):
            qh = jnp.where(first_head if hh == 0 else ~first_head, q, jnp.zeros_like(q))
            s = lax.dot_general(qh, k, (((1,), (1,)), ((), ())), preferred_element_type=F32)
            s = s + bias_ref[hh, :, K_WINDOW - width:]
            m = jnp.max(s, axis=-1, keepdims=True)
            p = jnp.exp(s - m)
            denom = jnp.sum(p, axis=-1, keepdims=True)
            outs.append(_dot(p.astype(BF16), v) / denom)
        o_ref[q0:q0 + Q_TILE, :] = jnp.where(first_head, outs[0], outs[1]).astype(BF16)


def _attention(qkv, bias, bsz, seq):
    pairs = ATT_WIDTH // V7X_LANES
    qkv3 = qkv.reshape(bsz, seq, 3 * ATT_WIDTH)

    def col_block(offset):
        return pl.BlockSpec((None, seq, V7X_LANES), lambda b, j: (b, 0, offset + j))

    out = pl.pallas_call(
        _attn_kernel,
        grid=(bsz, pairs),
        in_specs=[col_block(0), col_block(pairs), col_block(2 * pairs),
                  pl.BlockSpec((2, Q_TILE, K_WINDOW), lambda b, j: (j, 0, 0))],
        out_specs=pl.BlockSpec((None, seq, V7X_LANES), lambda b, j: (b, 0, j)),
        out_shape=jax.ShapeDtypeStruct((bsz, seq, ATT_WIDTH), BF16),
        compiler_params=_params(2),
        name="band_attention",
    )(qkv3, qkv3, qkv3, bias)
    return out.reshape(bsz * seq, ATT_WIDTH)


def _attention_bias(rel_bias):
    qi = jnp.arange(Q_TILE)[:, None]
    km = jnp.arange(K_WINDOW)[None, :]
    rel = jnp.clip(N_PREV_CHUNKS * CHUNK + qi - km, -REL_CLIP, REL_CLIP) + REL_CLIP
    qc, kc = qi // CHUNK, km // CHUNK
    in_band = (kc >= qc) & (kc <= qc + N_PREV_CHUNKS)
    return jnp.where(in_band[None], rel_bias.astype(F32)[:, rel], NEG_INF)


def _cmul(ar, ai, br, bi):
    return ar * br - ai * bi, ar * bi + ai * br


def _ssm_kernel(us_ref, bbd_ref, cbd_ref, lam_ref, mu_ref, d_ref, wglu_ref, bglu_ref, o_ref):
    n_seq, n_chunk, n_t, width = us_ref.shape
    rows = n_seq * n_chunk
    half = SSM_GROUPS * SSM_STATE
    lam_r, lam_i = lam_ref[0:1, :], lam_ref[1:2, :]

    def drive(s):
        u = us_ref[:, :, s, :].reshape(rows, width)
        bu = _dot(u.astype(BF16), bbd_ref[...])
        return u, bu[:, :half], bu[:, half:]

    er = jnp.zeros((rows, half), F32)
    ei = jnp.zeros((rows, half), F32)
    for s in range(n_t):
        _, br, bi = drive(s)
        er, ei = _cmul(er, ei, lam_r, lam_i)
        er, ei = er + br, ei + bi

    chunk_id = lax.broadcasted_iota(jnp.int32, (rows, half), 0) % n_chunk
    step = 1
    k = 0
    while step < n_chunk:
        pr = jnp.where(chunk_id >= step, pltpu.roll(er, step, 0), 0.0)
        pi = jnp.where(chunk_id >= step, pltpu.roll(ei, step, 0), 0.0)
        pr, pi = _cmul(pr, pi, mu_ref[2 * k:2 * k + 1, :], mu_ref[2 * k + 1:2 * k + 2, :])
        er, ei = er + pr, ei + pi
        step *= 2
        k += 1
    sr = jnp.where(chunk_id >= 1, pltpu.roll(er, 1, 0), 0.0)
    si = jnp.where(chunk_id >= 1, pltpu.roll(ei, 1, 0), 0.0)

    for s in range(n_t):
        u, br, bi = drive(s)
        sr, si = _cmul(sr, si, lam_r, lam_i)
        sr, si = sr + br, si + bi
        state = jnp.concatenate([sr, si], axis=-1).astype(BF16)
        y = _dot(state, cbd_ref[...]) + d_ref[...] * u
        g = _gelu_tanh(y)
        gate = jax.nn.sigmoid(_dot(g.astype(BF16), wglu_ref[...]) + bglu_ref[...])
        o_ref[:, :, s, :] = (g * gate).reshape(n_seq, n_chunk, width)


def _ssm(us, mats, bsz, seq):
    bbd, cbd, lam, mu, d, wglu, bglu = mats
    n_chunk = seq // SSM_T
    n_seq = min(SSM_SEQS, bsz)
    us4 = us.reshape(bsz, n_chunk, SSM_T, SSM_WIDTH)
    blk = pl.BlockSpec((n_seq, n_chunk, SSM_T, SSM_WIDTH), lambda i: (i, 0, 0, 0))
    out = pl.pallas_call(
        _ssm_kernel,
        grid=(bsz // n_seq,),
        in_specs=[blk] + [_resident(m.shape) for m in mats],
        out_specs=blk,
        out_shape=jax.ShapeDtypeStruct(us4.shape, F32),
        compiler_params=_params(1),
        name="s5_scan",
    )(us4, bbd, cbd, lam, mu, d, wglu, bglu)
    return out.reshape(bsz * seq, SSM_WIDTH)


def _ssm_matrices(a_re, a_im, log_dt, b_re, b_im, c_re, c_im, d, w_glu, b_glu, n_chunk):
    dt = jnp.exp(log_dt.astype(F32))[:, None]
    a_re = a_re.astype(F32)
    a_im = a_im.astype(F32)
    mag = jnp.exp(dt * a_re)
    ang = dt * a_im
    ab_re = mag * jnp.cos(ang)
    ab_im = mag * jnp.sin(ang)
    den = a_re * a_re + a_im * a_im
    nr = ab_re - 1.0
    ni = ab_im
    coef_re = (nr * a_re + ni * a_im) / den
    coef_im = (ni * a_re - nr * a_im) / den
    b_re = b_re.astype(F32)
    b_im = b_im.astype(F32)
    bb_re = coef_re[..., None] * b_re - coef_im[..., None] * b_im
    bb_im = coef_re[..., None] * b_im + coef_im[..., None] * b_re
    eye = jnp.eye(SSM_GROUPS, dtype=F32)
    half = SSM_GROUPS * SSM_STATE

    def drive_bd(m):
        return jnp.einsum('gph,gk->ghkp', m, eye).reshape(SSM_WIDTH, half)

    def read_bd(m):
        return jnp.einsum('ghp,gk->gpkh', m, eye).reshape(half, SSM_WIDTH)

    bbd = jnp.concatenate([drive_bd(bb_re), drive_bd(bb_im)], axis=1).astype(BF16)
    cbd = jnp.concatenate([read_bd(c_re.astype(F32)), -read_bd(c_im.astype(F32))], axis=0).astype(BF16)
    lam_r, lam_i = ab_re.reshape(1, half), ab_im.reshape(1, half)
    lam = jnp.concatenate([lam_r, lam_i], axis=0)
    pr, pi = lam_r, lam_i
    t = 1
    while t < SSM_T:
        pr, pi = _cmul(pr, pi, pr, pi)
        t *= 2
    mus = []
    step = 1
    while step < n_chunk:
        mus += [pr, pi]
        pr, pi = _cmul(pr, pi, pr, pi)
        step *= 2
    mu = jnp.concatenate(mus, axis=0)
    return (bbd, cbd, lam, mu, d.astype(F32).reshape(1, SSM_WIDTH), w_glu.astype(BF16),
            b_glu.astype(F32).reshape(1, SSM_WIDTH))


def _ab_out_kernel(x_ref, ya_ref, yb_ref, wa_ref, wb_ref, g_ref, b_ref, o_ref):
    mix = _dot(ya_ref[...], wa_ref[...]) + _dot(yb_ref[...].astype(BF16), wb_ref[...])
    o_ref[...] = _layer_norm(ALPHA * x_ref[...] + mix, g_ref[...], b_ref[...])


def _ab_out(x, ya, yb, wa, wb, g, b):
    n = x.shape[0]

    def rows(width):
        return pl.BlockSpec((ROW_TILE, width), lambda i: (i, 0))

    return pl.pallas_call(
        _ab_out_kernel,
        grid=(n // ROW_TILE,),
        in_specs=[rows(D_MODEL), rows(ATT_WIDTH), rows(SSM_WIDTH), _resident(wa.shape), _resident(wb.shape),
                  _resident((1, D_MODEL)), _resident((1, D_MODEL))],
        out_specs=rows(D_MODEL),
        out_shape=jax.ShapeDtypeStruct((n, D_MODEL), F32),
        compiler_params=_params(1),
        name="ab_out_ln",
    )(x, ya, yb, wa, wb, g, b)


def _cd_kernel(seq, x_ref, halo_ref, win_ref, pw_ref, ps_ref, sg_ref, sb_ref, ws_ref, bs_ref,
               wout_ref, g_ref, b_ref, o_ref):
    x = x_ref[...]
    h = _dot(x.astype(BF16), win_ref[...])
    xc = h[:, :POOL_WIDTH]
    row0 = (pl.program_id(0) * ROW_TILE) % seq
    hc = _dot(halo_ref[...].astype(BF16), win_ref[:, :POOL_WIDTH])
    hc = jnp.where(row0 == 0, 0.0, hc)
    ext = jnp.concatenate([hc, xc], axis=0)
    pos = row0 + lax.broadcasted_iota(jnp.int32, (ROW_TILE, POOL_GROUP), 0)
    acc = ext
    ys = []
    for gi, w in enumerate(POOL_WINDOWS):
        acc = acc + pltpu.roll(acc, w // 2, 0)
        cnt = jnp.minimum(pos + 1, w).astype(F32)
        pooled = acc[POOL_HALO:, :POOL_GROUP] / cnt - xc[:, gi * POOL_GROUP:(gi + 1) * POOL_GROUP]
        ys.append(_dot(pooled.astype(BF16), pw_ref[gi]))
        if gi + 1 < len(POOL_WINDOWS):
            acc = acc[:, POOL_GROUP:]
    yc = jnp.concatenate(ys, axis=-1) * ps_ref[...]

    z = _gelu_tanh(h[:, POOL_WIDTH:])
    u = z[:, :SGU_WIDTH]
    v = _layer_norm(z[:, SGU_WIDTH:], sg_ref[...], sb_ref[...]).astype(BF16)
    hd = SGU_WIDTH // SGU_HEADS
    blocks = []
    for r in range(ROW_TILE // SGU_CHUNK):
        rs = slice(r * SGU_CHUNK, (r + 1) * SGU_CHUNK)
        heads = [_dot(ws_ref[hh], v[rs, hh * hd:(hh + 1) * hd]) for hh in range(SGU_HEADS)]
        blocks.append(jnp.concatenate(heads, axis=-1) + bs_ref[...])
    yd = u * jnp.concatenate(blocks, axis=0)

    mix = _dot(yc.astype(BF16), wout_ref[:POOL_WIDTH, :]) + _dot(yd.astype(BF16), wout_ref[POOL_WIDTH:, :])
    o_ref[...] = _layer_norm(ALPHA * x + mix, g_ref[...], b_ref[...])


def _cd_mixer(x, seq, win, pw, ps, sg, sb, ws, bs, wout, g, b):
    n = x.shape[0]
    halo_per_tile = ROW_TILE // POOL_HALO
    row = pl.BlockSpec((ROW_TILE, D_MODEL), lambda i: (i, 0))
    halo = pl.BlockSpec((POOL_HALO, D_MODEL), lambda i: (jnp.maximum(i * halo_per_tile - 1, 0), 0))
    consts = (win, pw, ps, sg, sb, ws, bs, wout, g, b)
    return pl.pallas_call(
        functools.partial(_cd_kernel, seq),
        grid=(n // ROW_TILE,),
        in_specs=[row, halo] + [_resident(c.shape) for c in consts],
        out_specs=row,
        out_shape=jax.ShapeDtypeStruct((n, D_MODEL), F32),
        compiler_params=_params(1),
        name="cd_mixer_ln",
    )(x, x, *consts)


def kernel(x, ln_g, ln_b, ffn_w_gate, ffn_w_up, ffn_w_down, ab_w_in, ab_w_out, att_rel_bias, ssm_a_re, ssm_a_im, ssm_log_dt, ssm_b_re, ssm_b_im, ssm_c_re, ssm_c_im, ssm_d, ssm_w_glu, ssm_b_glu, cd_w_in, cd_w_out, pool_w, pool_scale, sgu_ln_g, sgu_ln_b, sgu_w_s, sgu_b_s):
    bsz, seq, d_model = x.shape
    assert d_model == D_MODEL and seq % ROW_TILE == 0 and seq % Q_TILE == 0 and seq % SSM_T == 0
    assert bsz % min(SSM_SEQS, bsz) == 0 and ln_g.shape[0] == DEPTH
    n = bsz * seq
    h = x.reshape(n, D_MODEL).astype(F32)

    def ln_params(l, j):
        return ln_g[l, j].astype(F32).reshape(1, D_MODEL), ln_b[l, j].astype(F32).reshape(1, D_MODEL)

    def ffn(h, l, j):
        return _ffn_ln(h, ffn_w_gate[l, j].astype(BF16), ffn_w_up[l, j].astype(BF16),
                       ffn_w_down[l, j].astype(BF16), *ln_params(l, j if j == 0 else 2))

    q_scale = jnp.concatenate([jnp.full((ATT_WIDTH,), ATT_HEAD_DIM ** -0.5, F32),
                               jnp.ones((2 * ATT_WIDTH + SSM_WIDTH,), F32)])
    tril = jnp.tril(jnp.ones((SGU_CHUNK, SGU_CHUNK), F32))

    for l in range(DEPTH):
        h = ffn(h, l, 0)
        i = l // 2
        if l % 2 == 0:
            qkv, us = _ab_in(h, (ab_w_in[i].astype(F32) * q_scale).astype(BF16))
            ya = _attention(qkv, _attention_bias(att_rel_bias[i]), bsz, seq)
            mats = _ssm_matrices(ssm_a_re[i], ssm_a_im[i], ssm_log_dt[i], ssm_b_re[i], ssm_b_im[i],
                                 ssm_c_re[i], ssm_c_im[i], ssm_d[i], ssm_w_glu[i], ssm_b_glu[i], seq // SSM_T)
            yb = _ssm(us, mats, bsz, seq)
            w_out = ab_w_out[i].astype(BF16)
            h = _ab_out(h, ya, yb, w_out[:ATT_WIDTH], w_out[ATT_WIDTH:], *ln_params(l, 1))
        else:
            bs_tbl = jnp.repeat(sgu_b_s[i].astype(F32).T, SGU_WIDTH // SGU_HEADS, axis=1)
            h = _cd_mixer(h, seq, cd_w_in[i].astype(BF16), pool_w[i].astype(BF16),
                          pool_scale[i].astype(F32).reshape(1, POOL_WIDTH),
                          sgu_ln_g[i].astype(F32).reshape(1, SGU_WIDTH), sgu_ln_b[i].astype(F32).reshape(1, SGU_WIDTH),
                          (sgu_w_s[i].astype(F32) * tril).astype(BF16), bs_tbl,
                          cd_w_out[i].astype(BF16), *ln_params(l, 1))
        h = ffn(h, l, 1)
    return h.reshape(bsz, seq, D_MODEL).astype(x.dtype)
```

```python
import functools

import jax
import jax.numpy as jnp
from jax import lax
from jax.experimental import pallas as pl
from jax.experimental.pallas import tpu as pltpu

BF16 = jnp.bfloat16
F32 = jnp.float32

DEPTH = 4
D_MODEL = 1024
D_FF = 2816
ALPHA = (2 * DEPTH) ** 0.25
LN_EPS = 1e-5
NEG_INF = -1e30
LOG2_E = 1.4426950408889634

CHUNK = 64
N_PREV_CHUNKS = 8
REL_CLIP = 128
ATT_HEAD_DIM = 64
ATT_WIDTH = 768
ATT_HEADS = 12
SSM_WIDTH = 256
SSM_GROUP = 16
SSM_GROUPS = 16
SSM_STATE = 64
POOL_WINDOWS = (2, 4, 8, 16)
POOL_WIDTH = 512
POOL_GROUP = 128
SGU_WIDTH = 512
SGU_CHUNK = 128
SGU_HEADS = 4

V7X_LANES = 128
V7X_MXU_DIM = 256
V7X_VMEM_LIMIT_BYTES = 60000 * 1024

ROW_TILE = 512
FF_CHUNK = V7X_MXU_DIM
Q_TILE = 4 * CHUNK
K_WINDOW = Q_TILE + N_PREV_CHUNKS * CHUNK
SSM_T = 8
SSM_SEQS = 2
POOL_HALO = 16


def _params(n_axes):
    return pltpu.CompilerParams(
        dimension_semantics=("arbitrary",) * n_axes,
        vmem_limit_bytes=V7X_VMEM_LIMIT_BYTES)


def _resident(shape):
    nd = len(shape)
    return pl.BlockSpec(shape, lambda *_: (0,) * nd, pipeline_mode=pl.Buffered(1))


def _layer_norm(z, g, b):
    mu = jnp.mean(z, axis=-1, keepdims=True)
    zc = z - mu
    var = jnp.mean(zc * zc, axis=-1, keepdims=True)
    return zc * lax.rsqrt(var + LN_EPS) * g + b


def _gelu_tanh(x):
    return 0.5 * x * (1.0 + jnp.tanh(0.7978845608028654 * (x + 0.044715 * (x * x * x))))


def _dot(a, b):
    return jnp.dot(a, b, preferred_element_type=F32)


def _ffn_kernel(x_ref, wg_ref, wu_ref, wd_ref, g_ref, b_ref, o_ref, h_scr):
    x = x_ref[...]
    xb = x.astype(BF16)
    for c in range(D_FF // FF_CHUNK):
        sl = slice(c * FF_CHUNK, (c + 1) * FF_CHUNK)
        gate = _dot(xb, wg_ref[:, sl])
        up = _dot(xb, wu_ref[:, sl])
        h_scr[:, sl] = (gate * jax.nn.sigmoid(gate) * up).astype(BF16)
    y = _dot(h_scr[...], wd_ref[...])
    o_ref[...] = _layer_norm(ALPHA * x + 0.5 * y, g_ref[...], b_ref[...])


def _ffn_ln(x, wg, wu, wd, g, b):
    n = x.shape[0]
    row = pl.BlockSpec((ROW_TILE, D_MODEL), lambda i: (i, 0))
    return pl.pallas_call(
        _ffn_kernel,
        grid=(n // ROW_TILE,),
        in_specs=[row, _resident((D_MODEL, D_FF)), _resident((D_MODEL, D_FF)),
                  _resident((D_FF, D_MODEL)), _resident((1, D_MODEL)), _resident((1, D_MODEL))],
        out_specs=row,
        out_shape=jax.ShapeDtypeStruct((n, D_MODEL), F32),
        scratch_shapes=[pltpu.VMEM((ROW_TILE, D_FF), BF16)],
        compiler_params=_params(1),
        name="ffn_ln",
    )(x, wg, wu, wd, g, b)


def _ab_in_kernel(x_ref, w_ref, qkv_ref, us_ref):
    h = _dot(x_ref[...].astype(BF16), w_ref[...])
    qkv_ref[...] = h[:, :3 * ATT_WIDTH].astype(BF16)
    us_ref[...] = h[:, 3 * ATT_WIDTH:]


def _ab_in(x, w):
    n = x.shape[0]
    width = 3 * ATT_WIDTH + SSM_WIDTH
    return pl.pallas_call(
        _ab_in_kernel,
        grid=(n // ROW_TILE,),
        in_specs=[pl.BlockSpec((ROW_TILE, D_MODEL), lambda i: (i, 0)), _resident((D_MODEL, width))],
        out_specs=[pl.BlockSpec((ROW_TILE, 3 * ATT_WIDTH), lambda i: (i, 0)),
                   pl.BlockSpec((ROW_TILE, SSM_WIDTH), lambda i: (i, 0))],
        out_shape=[jax.ShapeDtypeStruct((n, 3 * ATT_WIDTH), BF16),
                   jax.ShapeDtypeStruct((n, SSM_WIDTH), F32)],
        compiler_params=_params(1),
        name="ab_in",
    )(x, w)


def _attn_kernel(q_ref, k_ref, v_ref, bias_ref, o_ref):
    seq = q_ref.shape[0]
    lane = lax.broadcasted_iota(jnp.int32, (Q_TILE, V7X_LANES), 1)
    first_head = lane < ATT_HEAD_DIM
    for t in range(seq // Q_TILE):
        q0 = t * Q_TILE
        k0 = max(0, q0 + Q_TILE - K_WINDOW)
        width = q0 + Q_TILE - k0
        q = q_ref[q0:q0 + Q_TILE, :]
        k = k_ref[k0:k0 + width, :]
        v = v_ref[k0:k0 + width, :]
        first_head_kv = lax.broadcasted_iota(jnp.int32, (width, V7X_LANES), 1) < ATT_HEAD_DIM
        outs = []
        for hh in range(2):
            mine = first_head if hh == 0 else ~first_head
            qh = jnp.where(mine, q, jnp.zeros_like(q))
            vh = jnp.where(first_head_kv if hh == 0 else ~first_head_kv, v, jnp.ones_like(v))
            s = lax.dot_general(qh, k, (((1,), (1,)), ((), ())), preferred_element_type=F32)
            s = s + bias_ref[hh, :, K_WINDOW - width:]
            m = jnp.max(s, axis=-1, keepdims=True)
            outs.append(_dot(jnp.exp2(s - m).astype(BF16), vh))
        num = jnp.where(first_head, outs[0], outs[1])
        den = pltpu.roll(jnp.where(first_head, outs[1], outs[0]), ATT_HEAD_DIM, 1)
        o_ref[q0:q0 + Q_TILE, :] = (num / den).astype(BF16)


def _attention(qkv, bias, bsz, seq):
    pairs = ATT_WIDTH // V7X_LANES
    qkv3 = qkv.reshape(bsz, seq, 3 * ATT_WIDTH)

    def col_block(offset):
        return pl.BlockSpec((None, seq, V7X_LANES), lambda b, j: (b, 0, offset + j))

    out = pl.pallas_call(
        _attn_kernel,
        grid=(bsz, pairs),
        in_specs=[col_block(0), col_block(pairs), col_block(2 * pairs),
                  pl.BlockSpec((2, Q_TILE, K_WINDOW), lambda b, j: (j, 0, 0))],
        out_specs=pl.BlockSpec((None, seq, V7X_LANES), lambda b, j: (b, 0, j)),
        out_shape=jax.ShapeDtypeStruct((bsz, seq, ATT_WIDTH), BF16),
        compiler_params=_params(2),
        name="band_attention",
    )(qkv3, qkv3, qkv3, bias)
    return out.reshape(bsz * seq, ATT_WIDTH)


def _attention_bias(rel_bias):
    heads = rel_bias.shape[0]
    rb = rel_bias.astype(F32) * LOG2_E
    d_min = N_PREV_CHUNKS * CHUNK - (K_WINDOW - 1)
    d_max = N_PREV_CHUNKS * CHUNK + Q_TILE - 1
    n = d_max - d_min + 1
    by_dist = jnp.concatenate([jnp.repeat(rb[:, :1], -REL_CLIP - d_min, axis=1), rb,
                               jnp.repeat(rb[:, -1:], d_max - REL_CLIP, axis=1)], axis=1)
    rev = jnp.concatenate([by_dist[:, ::-1], jnp.zeros((heads, 1), F32)], axis=1)
    skew = jnp.tile(rev, (1, Q_TILE))[:, :Q_TILE * n].reshape(heads, Q_TILE, n)
    table = skew[:, :, Q_TILE - 1:Q_TILE - 1 + K_WINDOW]
    qc = jnp.arange(Q_TILE)[:, None] // CHUNK
    kc = jnp.arange(K_WINDOW)[None, :] // CHUNK
    in_band = (kc >= qc) & (kc <= qc + N_PREV_CHUNKS)
    return jnp.where(in_band[None], table, NEG_INF)


def _cmul(ar, ai, br, bi):
    return ar * br - ai * bi, ar * bi + ai * br


def _ssm_kernel(n_chunk, x_ref, mloc_ref, we_ref, wc_ref, mu_ref, o_ref):
    rows = x_ref.shape[0]
    half = SSM_GROUPS * SSM_STATE
    xb = x_ref[...].astype(BF16)
    e = _dot(xb, we_ref[...])
    er, ei = e[:, :half], e[:, half:]
    chunk_id = lax.broadcasted_iota(jnp.int32, (rows, half), 0) % n_chunk
    step = 1
    k = 0
    while step < n_chunk:
        pr = jnp.where(chunk_id >= step, pltpu.roll(er, step, 0), 0.0)
        pi = jnp.where(chunk_id >= step, pltpu.roll(ei, step, 0), 0.0)
        pr, pi = _cmul(pr, pi, mu_ref[2 * k:2 * k + 1, :], mu_ref[2 * k + 1:2 * k + 2, :])
        er, ei = er + pr, ei + pi
        step *= 2
        k += 1
    sr = jnp.where(chunk_id >= 1, pltpu.roll(er, 1, 0), 0.0)
    si = jnp.where(chunk_id >= 1, pltpu.roll(ei, 1, 0), 0.0)
    state = jnp.concatenate([sr, si], axis=-1).astype(BF16)
    o_ref[...] = _dot(xb, mloc_ref[...]) + _dot(state, wc_ref[...])


def _ssm(us, mats, bsz, seq):
    mloc, we, wc, mu = mats
    n_chunk = seq // SSM_T
    n_seq = min(SSM_SEQS, bsz)
    rows = n_seq * n_chunk
    x = us.reshape(bsz * n_chunk, SSM_T * SSM_WIDTH)
    blk = pl.BlockSpec((rows, SSM_T * SSM_WIDTH), lambda i: (i, 0))
    out = pl.pallas_call(
        functools.partial(_ssm_kernel, n_chunk),
        grid=(bsz // n_seq,),
        in_specs=[blk] + [_resident(m.shape) for m in mats],
        out_specs=blk,
        out_shape=jax.ShapeDtypeStruct(x.shape, F32),
        compiler_params=_params(1),
        name="s5_scan",
    )(x, mloc, we, wc, mu)
    return out.reshape(bsz * seq, SSM_WIDTH)


def _ssm_matrices(a_re, a_im, log_dt, b_re, b_im, c_re, c_im, n_chunk):
    hi = lax.Precision.HIGHEST
    dt = jnp.exp(log_dt.astype(F32))[:, None]
    a_re = a_re.astype(F32)
    a_im = a_im.astype(F32)
    mag = jnp.exp(dt * a_re)
    ang = dt * a_im
    ab_re = mag * jnp.cos(ang)
    ab_im = mag * jnp.sin(ang)
    den = a_re * a_re + a_im * a_im
    nr = ab_re - 1.0
    ni = ab_im
    coef_re = (nr * a_re + ni * a_im) / den
    coef_im = (ni * a_re - nr * a_im) / den
    b_re = b_re.astype(F32)
    b_im = b_im.astype(F32)
    bb_re = coef_re[..., None] * b_re - coef_im[..., None] * b_im
    bb_im = coef_re[..., None] * b_im + coef_im[..., None] * b_re
    c_re = c_re.astype(F32)
    c_im = c_im.astype(F32)
    pw_r, pw_i = [jnp.ones_like(ab_re)], [jnp.zeros_like(ab_im)]
    for _ in range(SSM_T):
        r, i = _cmul(pw_r[-1], pw_i[-1], ab_re, ab_im)
        pw_r.append(r)
        pw_i.append(i)
    pw_r, pw_i = jnp.stack(pw_r), jnp.stack(pw_i)
    eye = jnp.eye(SSM_GROUPS, dtype=F32)
    half = SSM_GROUPS * SSM_STATE
    width = SSM_T * SSM_WIDTH

    cp_r = c_re[None] * pw_r[:, :, None, :] - c_im[None] * pw_i[:, :, None, :]
    cp_i = c_re[None] * pw_i[:, :, None, :] + c_im[None] * pw_r[:, :, None, :]
    kern = (jnp.einsum('tghp,gpk->tghk', cp_r[:SSM_T], bb_re, precision=hi)
            - jnp.einsum('tghp,gpk->tghk', cp_i[:SSM_T], bb_im, precision=hi))
    lag = jnp.arange(SSM_T)[None, :] - jnp.arange(SSM_T)[:, None]
    onehot = (lag[None] == jnp.arange(SSM_T)[:, None, None]).astype(F32)
    kst = jnp.einsum('xst,xghk->sgkth', onehot, kern, precision=hi)
    mloc = (kst[:, :, :, :, None, :] * eye[None, :, None, None, :, None]).reshape(width, width)

    rev_r, rev_i = pw_r[SSM_T - 1::-1], pw_i[SSM_T - 1::-1]
    pb_r = rev_r[..., None] * bb_re[None] - rev_i[..., None] * bb_im[None]
    pb_i = rev_r[..., None] * bb_im[None] + rev_i[..., None] * bb_re[None]

    def drive(m):
        m = jnp.transpose(m, (0, 1, 3, 2))
        return (m[:, :, :, None, :] * eye[None, :, None, :, None]).reshape(width, half)

    we = jnp.concatenate([drive(pb_r), drive(pb_i)], axis=1)

    def read(m):
        m = jnp.transpose(m[1:], (1, 3, 0, 2))
        return (m[:, :, :, None, :] * eye[:, None, None, :, None]).reshape(half, width)

    wc = jnp.concatenate([read(cp_r), -read(cp_i)], axis=0)

    pr, pi = pw_r[SSM_T].reshape(1, half), pw_i[SSM_T].reshape(1, half)
    mus = []
    step = 1
    while step < n_chunk:
        mus += [pr, pi]
        pr, pi = _cmul(pr, pi, pr, pi)
        step *= 2
    return mloc.astype(BF16), we.astype(BF16), wc.astype(BF16), jnp.concatenate(mus, axis=0)


def _ab_out_kernel(x_ref, ya_ref, ys_ref, us_ref, d_ref, wglu_ref, bglu_ref, wa_ref, wb_ref, g_ref, b_ref, o_ref):
    y = ys_ref[...] + d_ref[...] * us_ref[...]
    gl = _gelu_tanh(y)
    yb = gl * jax.nn.sigmoid(_dot(gl.astype(BF16), wglu_ref[...]) + bglu_ref[...])
    mix = _dot(ya_ref[...], wa_ref[...]) + _dot(yb.astype(BF16), wb_ref[...])
    o_ref[...] = _layer_norm(ALPHA * x_ref[...] + mix, g_ref[...], b_ref[...])


def _ab_out(x, ya, ys, us, d, wglu, bglu, wa, wb, g, b):
    n = x.shape[0]

    def rows(width):
        return pl.BlockSpec((ROW_TILE, width), lambda i: (i, 0))

    consts = (d, wglu, bglu, wa, wb, g, b)
    return pl.pallas_call(
        _ab_out_kernel,
        grid=(n // ROW_TILE,),
        in_specs=[rows(D_MODEL), rows(ATT_WIDTH), rows(SSM_WIDTH), rows(SSM_WIDTH)] + [_resident(c.shape) for c in consts],
        out_specs=rows(D_MODEL),
        out_shape=jax.ShapeDtypeStruct((n, D_MODEL), F32),
        compiler_params=_params(1),
        name="ab_out_ln",
    )(x, ya, ys, us, *consts)


def _cd_kernel(seq, x_ref, halo_ref, win_ref, pw_ref, ps_ref, sg_ref, sb_ref, ws_ref, bs_ref,
               wout_ref, g_ref, b_ref, o_ref):
    x = x_ref[...]
    h = _dot(x.astype(BF16), win_ref[...])
    xc = h[:, :POOL_WIDTH]
    row0 = (pl.program_id(0) * ROW_TILE) % seq
    hc = _dot(halo_ref[...].astype(BF16), win_ref[:, :POOL_WIDTH])
    hc = jnp.where(row0 == 0, 0.0, hc)
    ext = jnp.concatenate([hc, xc], axis=0)
    pos = row0 + lax.broadcasted_iota(jnp.int32, (ROW_TILE, POOL_GROUP), 0)
    acc = ext
    ys = []
    for gi, w in enumerate(POOL_WINDOWS):
        acc = acc + pltpu.roll(acc, w // 2, 0)
        cnt = jnp.minimum(pos + 1, w).astype(F32)
        pooled = acc[POOL_HALO:, :POOL_GROUP] / cnt - xc[:, gi * POOL_GROUP:(gi + 1) * POOL_GROUP]
        ys.append(_dot(pooled.astype(BF16), pw_ref[gi]))
        if gi + 1 < len(POOL_WINDOWS):
            acc = acc[:, POOL_GROUP:]
    yc = jnp.concatenate(ys, axis=-1) * ps_ref[...]

    z = _gelu_tanh(h[:, POOL_WIDTH:])
    u = z[:, :SGU_WIDTH]
    v = _layer_norm(z[:, SGU_WIDTH:], sg_ref[...], sb_ref[...]).astype(BF16)
    hd = SGU_WIDTH // SGU_HEADS
    blocks = []
    for r in range(ROW_TILE // SGU_CHUNK):
        rs = slice(r * SGU_CHUNK, (r + 1) * SGU_CHUNK)
        heads = [_dot(ws_ref[hh], v[rs, hh * hd:(hh + 1) * hd]) for hh in range(SGU_HEADS)]
        blocks.append(jnp.concatenate(heads, axis=-1) + bs_ref[...])
    yd = u * jnp.concatenate(blocks, axis=0)

    mix = _dot(yc.astype(BF16), wout_ref[:POOL_WIDTH, :]) + _dot(yd.astype(BF16), wout_ref[POOL_WIDTH:, :])
    o_ref[...] = _layer_norm(ALPHA * x + mix, g_ref[...], b_ref[...])


def _cd_mixer(x, seq, win, pw, ps, sg, sb, ws, bs, wout, g, b):
    n = x.shape[0]
    halo_per_tile = ROW_TILE // POOL_HALO
    row = pl.BlockSpec((ROW_TILE, D_MODEL), lambda i: (i, 0))
    halo = pl.BlockSpec((POOL_HALO, D_MODEL), lambda i: (jnp.maximum(i * halo_per_tile - 1, 0), 0))
    consts = (win, pw, ps, sg, sb, ws, bs, wout, g, b)
    return pl.pallas_call(
        functools.partial(_cd_kernel, seq),
        grid=(n // ROW_TILE,),
        in_specs=[row, halo] + [_resident(c.shape) for c in consts],
        out_specs=row,
        out_shape=jax.ShapeDtypeStruct((n, D_MODEL), F32),
        compiler_params=_params(1),
        name="cd_mixer_ln",
    )(x, x, *consts)


def kernel(x, ln_g, ln_b, ffn_w_gate, ffn_w_up, ffn_w_down, ab_w_in, ab_w_out, att_rel_bias, ssm_a_re, ssm_a_im, ssm_log_dt, ssm_b_re, ssm_b_im, ssm_c_re, ssm_c_im, ssm_d, ssm_w_glu, ssm_b_glu, cd_w_in, cd_w_out, pool_w, pool_scale, sgu_ln_g, sgu_ln_b, sgu_w_s, sgu_b_s):
    bsz, seq, d_model = x.shape
    assert d_model == D_MODEL and seq % ROW_TILE == 0 and seq % Q_TILE == 0 and seq % SSM_T == 0
    assert bsz % min(SSM_SEQS, bsz) == 0 and ln_g.shape[0] == DEPTH
    n = bsz * seq
    h = x.reshape(n, D_MODEL).astype(F32)

    def ln_params(l, j):
        return ln_g[l, j].astype(F32).reshape(1, D_MODEL), ln_b[l, j].astype(F32).reshape(1, D_MODEL)

    def ffn(h, l, j):
        return _ffn_ln(h, ffn_w_gate[l, j].astype(BF16), ffn_w_up[l, j].astype(BF16),
                       ffn_w_down[l, j].astype(BF16), *ln_params(l, j if j == 0 else 2))

    q_scale = jnp.concatenate([jnp.full((ATT_WIDTH,), ATT_HEAD_DIM ** -0.5 * LOG2_E, F32),
                               jnp.ones((2 * ATT_WIDTH + SSM_WIDTH,), F32)])
    tril = jnp.tril(jnp.ones((SGU_CHUNK, SGU_CHUNK), F32))

    for l in range(DEPTH):
        h = ffn(h, l, 0)
        i = l // 2
        if l % 2 == 0:
            qkv, us = _ab_in(h, (ab_w_in[i].astype(F32) * q_scale).astype(BF16))
            ya = _attention(qkv, _attention_bias(att_rel_bias[i]), bsz, seq)
            mats = _ssm_matrices(ssm_a_re[i], ssm_a_im[i], ssm_log_dt[i], ssm_b_re[i], ssm_b_im[i],
                                 ssm_c_re[i], ssm_c_im[i], seq // SSM_T)
            ys = _ssm(us, mats, bsz, seq)
            w_out = ab_w_out[i].astype(BF16)
            h = _ab_out(h, ya, ys, us, ssm_d[i].astype(F32).reshape(1, SSM_WIDTH), ssm_w_glu[i].astype(BF16),
                        ssm_b_glu[i].astype(F32).reshape(1, SSM_WIDTH), w_out[:ATT_WIDTH], w_out[ATT_WIDTH:],
                        *ln_params(l, 1))
        else:
            bs_tbl = jnp.repeat(sgu_b_s[i].astype(F32).T, SGU_WIDTH // SGU_HEADS, axis=1)
            h = _cd_mixer(h, seq, cd_w_in[i].astype(BF16), pool_w[i].astype(BF16),
                          pool_scale[i].astype(F32).reshape(1, POOL_WIDTH),
                          sgu_ln_g[i].astype(F32).reshape(1, SGU_WIDTH), sgu_ln_b[i].astype(F32).reshape(1, SGU_WIDTH),
                          (sgu_w_s[i].astype(F32) * tril).astype(BF16), bs_tbl,
                          cd_w_out[i].astype(BF16), *ln_params(l, 1))
        h = ffn(h, l, 1)
    return h.reshape(bsz, seq, D_MODEL).astype(x.dtype)
```

```python
import functools

import jax
import jax.numpy as jnp
from jax import lax
from jax.experimental import pallas as pl
from jax.experimental.pallas import tpu as pltpu

BF16 = jnp.bfloat16
F32 = jnp.float32

DEPTH = 4
D_MODEL = 1024
D_FF = 2816
ALPHA = (2 * DEPTH) ** 0.25
LN_EPS = 1e-5
NEG_INF = -1e30
LOG2_E = 1.4426950408889634

CHUNK = 64
N_PREV_CHUNKS = 8
REL_CLIP = 128
ATT_HEAD_DIM = 64
ATT_WIDTH = 768
ATT_HEADS = 12
SSM_WIDTH = 256
SSM_GROUP = 16
SSM_GROUPS = 16
SSM_STATE = 64
POOL_WINDOWS = (2, 4, 8, 16)
POOL_WIDTH = 512
POOL_GROUP = 128
SGU_WIDTH = 512
SGU_CHUNK = 128
SGU_HEADS = 4

V7X_LANES = 128
V7X_MXU_DIM = 256
V7X_VMEM_LIMIT_BYTES = 60000 * 1024

ROW_TILE = 512
FF_CHUNK = V7X_MXU_DIM
Q_TILE = 4 * CHUNK
K_WINDOW = Q_TILE + N_PREV_CHUNKS * CHUNK
SSM_T = 8
SSM_SEQS = 2
POOL_HALO = 16


def _params(n_axes):
    return pltpu.CompilerParams(
        dimension_semantics=("arbitrary",) * n_axes,
        vmem_limit_bytes=V7X_VMEM_LIMIT_BYTES)


def _resident(shape):
    nd = len(shape)
    return pl.BlockSpec(shape, lambda *_: (0,) * nd, pipeline_mode=pl.Buffered(1))


def _layer_norm(z, g, b):
    mu = jnp.mean(z, axis=-1, keepdims=True)
    zc = z - mu
    var = jnp.mean(zc * zc, axis=-1, keepdims=True)
    return zc * lax.rsqrt(var + LN_EPS) * g + b


def _gelu_tanh(x):
    return 0.5 * x * (1.0 + jnp.tanh(0.7978845608028654 * (x + 0.044715 * (x * x * x))))


def _dot(a, b):
    return jnp.dot(a, b, preferred_element_type=F32)


def _ffn_kernel(x_ref, wg_ref, wu_ref, wd_ref, g_ref, b_ref, o_ref, h_scr):
    x = x_ref[...]
    xb = x.astype(BF16)
    for c in range(D_FF // FF_CHUNK):
        sl = slice(c * FF_CHUNK, (c + 1) * FF_CHUNK)
        gate = _dot(xb, wg_ref[:, sl])
        up = _dot(xb, wu_ref[:, sl])
        h_scr[:, sl] = (gate * jax.nn.sigmoid(gate) * up).astype(BF16)
    y = _dot(h_scr[...], wd_ref[...])
    o_ref[...] = _layer_norm(ALPHA * x + 0.5 * y, g_ref[...], b_ref[...])


def _ffn_ln(x, wg, wu, wd, g, b):
    n = x.shape[0]
    row = pl.BlockSpec((ROW_TILE, D_MODEL), lambda i: (i, 0))
    return pl.pallas_call(
        _ffn_kernel,
        grid=(n // ROW_TILE,),
        in_specs=[row, _resident((D_MODEL, D_FF)), _resident((D_MODEL, D_FF)),
                  _resident((D_FF, D_MODEL)), _resident((1, D_MODEL)), _resident((1, D_MODEL))],
        out_specs=row,
        out_shape=jax.ShapeDtypeStruct((n, D_MODEL), F32),
        scratch_shapes=[pltpu.VMEM((ROW_TILE, D_FF), BF16)],
        compiler_params=_params(1),
        name="ffn_ln",
    )(x, wg, wu, wd, g, b)


def _ab_in_kernel(x_ref, w_ref, qkv_ref, us_ref, usb_ref):
    h = _dot(x_ref[...].astype(BF16), w_ref[...])
    qkv_ref[...] = h[:, :3 * ATT_WIDTH].astype(BF16)
    us = h[:, 3 * ATT_WIDTH:]
    us_ref[...] = us
    usb_ref[...] = us.astype(BF16)


def _ab_in(x, w):
    n = x.shape[0]
    width = 3 * ATT_WIDTH + SSM_WIDTH
    return pl.pallas_call(
        _ab_in_kernel,
        grid=(n // ROW_TILE,),
        in_specs=[pl.BlockSpec((ROW_TILE, D_MODEL), lambda i: (i, 0)), _resident((D_MODEL, width))],
        out_specs=[pl.BlockSpec((ROW_TILE, 3 * ATT_WIDTH), lambda i: (i, 0)),
                   pl.BlockSpec((ROW_TILE, SSM_WIDTH), lambda i: (i, 0)),
                   pl.BlockSpec((ROW_TILE, SSM_WIDTH), lambda i: (i, 0))],
        out_shape=[jax.ShapeDtypeStruct((n, 3 * ATT_WIDTH), BF16),
                   jax.ShapeDtypeStruct((n, SSM_WIDTH), F32),
                   jax.ShapeDtypeStruct((n, SSM_WIDTH), BF16)],
        compiler_params=_params(1),
        name="ab_in",
    )(x, w)


def _attn_kernel(q_ref, k_ref, v_ref, bias_ref, o_ref):
    seq = q_ref.shape[0]
    lane = lax.broadcasted_iota(jnp.int32, (Q_TILE, V7X_LANES), 1)
    first_head = lane < ATT_HEAD_DIM
    for t in range(seq // Q_TILE):
        q0 = t * Q_TILE
        k0 = max(0, q0 + Q_TILE - K_WINDOW)
        width = q0 + Q_TILE - k0
        q = q_ref[q0:q0 + Q_TILE, :]
        k = k_ref[k0:k0 + width, :]
        v = v_ref[k0:k0 + width, :]
        zero = jnp.zeros_like(q)
        qs = jnp.concatenate([jnp.where(first_head, q, zero), jnp.where(first_head, zero, q)], axis=0)
        s = lax.dot_general(qs, k, (((1,), (1,)), ((), ())), preferred_element_type=F32)
        s = s + bias_ref[:, :, K_WINDOW - width:].reshape(2 * Q_TILE, width)
        p = jnp.exp2(s - jnp.max(s, axis=-1, keepdims=True))
        o = _dot(p.astype(BF16), v) / jnp.sum(p, axis=-1, keepdims=True)
        o_ref[q0:q0 + Q_TILE, :] = jnp.where(first_head, o[:Q_TILE], o[Q_TILE:]).astype(BF16)


def _attention(qkv, bias, bsz, seq):
    pairs = ATT_WIDTH // V7X_LANES
    qkv3 = qkv.reshape(bsz, seq, 3 * ATT_WIDTH)

    def col_block(offset):
        return pl.BlockSpec((None, seq, V7X_LANES), lambda b, j: (b, 0, offset + j))

    out = pl.pallas_call(
        _attn_kernel,
        grid=(bsz, pairs),
        in_specs=[col_block(0), col_block(pairs), col_block(2 * pairs),
                  pl.BlockSpec((2, Q_TILE, K_WINDOW), lambda b, j: (j, 0, 0))],
        out_specs=pl.BlockSpec((None, seq, V7X_LANES), lambda b, j: (b, 0, j)),
        out_shape=jax.ShapeDtypeStruct((bsz, seq, ATT_WIDTH), BF16),
        compiler_params=_params(2),
        name="band_attention",
    )(qkv3, qkv3, qkv3, bias)
    return out.reshape(bsz * seq, ATT_WIDTH)


def _attention_bias(rel_bias):
    heads = rel_bias.shape[0]
    rb = rel_bias.astype(F32) * LOG2_E
    d_min = N_PREV_CHUNKS * CHUNK - (K_WINDOW - 1)
    d_max = N_PREV_CHUNKS * CHUNK + Q_TILE - 1
    n = d_max - d_min + 1
    by_dist = jnp.concatenate([jnp.repeat(rb[:, :1], -REL_CLIP - d_min, axis=1), rb,
                               jnp.repeat(rb[:, -1:], d_max - REL_CLIP, axis=1)], axis=1)
    rev = jnp.concatenate([by_dist[:, ::-1], jnp.zeros((heads, 1), F32)], axis=1)
    skew = jnp.tile(rev, (1, Q_TILE))[:, :Q_TILE * n].reshape(heads, Q_TILE, n)
    table = skew[:, :, Q_TILE - 1:Q_TILE - 1 + K_WINDOW]
    qc = jnp.arange(Q_TILE)[:, None] // CHUNK
    kc = jnp.arange(K_WINDOW)[None, :] // CHUNK
    in_band = (kc >= qc) & (kc <= qc + N_PREV_CHUNKS)
    return jnp.where(in_band[None], table, NEG_INF)


def _cmul(ar, ai, br, bi):
    return ar * br - ai * bi, ar * bi + ai * br


def _ssm_kernel(n_chunk, x_ref, mloc_ref, we_ref, wc_ref, mu_ref, o_ref):
    rows = x_ref.shape[0]
    half = SSM_GROUPS * SSM_STATE
    xb = x_ref[...]
    e = _dot(xb, we_ref[...])
    er, ei = e[:, :half], e[:, half:]
    chunk_id = lax.broadcasted_iota(jnp.int32, (rows, half), 0) % n_chunk
    step = 1
    k = 0
    while step < n_chunk:
        pr = jnp.where(chunk_id >= step, pltpu.roll(er, step, 0), 0.0)
        pi = jnp.where(chunk_id >= step, pltpu.roll(ei, step, 0), 0.0)
        pr, pi = _cmul(pr, pi, mu_ref[2 * k:2 * k + 1, :], mu_ref[2 * k + 1:2 * k + 2, :])
        er, ei = er + pr, ei + pi
        step *= 2
        k += 1
    sr = jnp.where(chunk_id >= 1, pltpu.roll(er, 1, 0), 0.0)
    si = jnp.where(chunk_id >= 1, pltpu.roll(ei, 1, 0), 0.0)
    state = jnp.concatenate([sr, si], axis=-1).astype(BF16)
    o_ref[...] = _dot(xb, mloc_ref[...]) + _dot(state, wc_ref[...])


def _ssm(us, mats, bsz, seq):
    mloc, we, wc, mu = mats
    n_chunk = seq // SSM_T
    n_seq = min(SSM_SEQS, bsz)
    rows = n_seq * n_chunk
    assert us.dtype == BF16
    x = us.reshape(bsz * n_chunk, SSM_T * SSM_WIDTH)
    blk = pl.BlockSpec((rows, SSM_T * SSM_WIDTH), lambda i: (i, 0))
    out = pl.pallas_call(
        functools.partial(_ssm_kernel, n_chunk),
        grid=(bsz // n_seq,),
        in_specs=[blk] + [_resident(m.shape) for m in mats],
        out_specs=blk,
        out_shape=jax.ShapeDtypeStruct(x.shape, F32),
        compiler_params=_params(1),
        name="s5_scan",
    )(x, mloc, we, wc, mu)
    return out.reshape(bsz * seq, SSM_WIDTH)


def _ssm_matrices(a_re, a_im, log_dt, b_re, b_im, c_re, c_im, n_chunk):
    hi = lax.Precision.HIGHEST
    dt = jnp.exp(log_dt.astype(F32))[:, None]
    a_re = a_re.astype(F32)
    a_im = a_im.astype(F32)
    mag = jnp.exp(dt * a_re)
    ang = dt * a_im
    ab_re = mag * jnp.cos(ang)
    ab_im = mag * jnp.sin(ang)
    den = a_re * a_re + a_im * a_im
    nr = ab_re - 1.0
    ni = ab_im
    coef_re = (nr * a_re + ni * a_im) / den
    coef_im = (ni * a_re - nr * a_im) / den
    b_re = b_re.astype(F32)
    b_im = b_im.astype(F32)
    bb_re = coef_re[..., None] * b_re - coef_im[..., None] * b_im
    bb_im = coef_re[..., None] * b_im + coef_im[..., None] * b_re
    c_re = c_re.astype(F32)
    c_im = c_im.astype(F32)
    pw_r, pw_i = [jnp.ones_like(ab_re)], [jnp.zeros_like(ab_im)]
    for _ in range(SSM_T):
        r, i = _cmul(pw_r[-1], pw_i[-1], ab_re, ab_im)
        pw_r.append(r)
        pw_i.append(i)
    pw_r, pw_i = jnp.stack(pw_r), jnp.stack(pw_i)
    eye = jnp.eye(SSM_GROUPS, dtype=F32)
    half = SSM_GROUPS * SSM_STATE
    width = SSM_T * SSM_WIDTH

    cp_r = c_re[None] * pw_r[:, :, None, :] - c_im[None] * pw_i[:, :, None, :]
    cp_i = c_re[None] * pw_i[:, :, None, :] + c_im[None] * pw_r[:, :, None, :]
    kern = (jnp.einsum('tghp,gpk->tghk', cp_r[:SSM_T], bb_re, precision=hi)
            - jnp.einsum('tghp,gpk->tghk', cp_i[:SSM_T], bb_im, precision=hi))
    lag = jnp.arange(SSM_T)[None, :] - jnp.arange(SSM_T)[:, None]
    onehot = (lag[None] == jnp.arange(SSM_T)[:, None, None]).astype(F32)
    kst = jnp.einsum('xst,xghk->sgkth', onehot, kern, precision=hi)
    mloc = (kst[:, :, :, :, None, :] * eye[None, :, None, None, :, None]).reshape(width, width)

    rev_r, rev_i = pw_r[SSM_T - 1::-1], pw_i[SSM_T - 1::-1]
    pb_r = rev_r[..., None] * bb_re[None] - rev_i[..., None] * bb_im[None]
    pb_i = rev_r[..., None] * bb_im[None] + rev_i[..., None] * bb_re[None]

    def drive(m):
        m = jnp.transpose(m, (0, 1, 3, 2))
        return (m[:, :, :, None, :] * eye[None, :, None, :, None]).reshape(width, half)

    we = jnp.concatenate([drive(pb_r), drive(pb_i)], axis=1)

    def read(m):
        m = jnp.transpose(m[1:], (1, 3, 0, 2))
        return (m[:, :, :, None, :] * eye[:, None, None, :, None]).reshape(half, width)

    wc = jnp.concatenate([read(cp_r), -read(cp_i)], axis=0)

    pr, pi = pw_r[SSM_T].reshape(1, half), pw_i[SSM_T].reshape(1, half)
    mus = []
    step = 1
    while step < n_chunk:
        mus += [pr, pi]
        pr, pi = _cmul(pr, pi, pr, pi)
        step *= 2
    return mloc.astype(BF16), we.astype(BF16), wc.astype(BF16), jnp.concatenate(mus, axis=0)


def _ab_out_kernel(x_ref, ya_ref, ys_ref, us_ref, d_ref, wglu_ref, bglu_ref, wa_ref, wb_ref, g_ref, b_ref, o_ref):
    y = ys_ref[...] + d_ref[...] * us_ref[...]
    gl = _gelu_tanh(y)
    yb = gl * jax.nn.sigmoid(_dot(gl.astype(BF16), wglu_ref[...]) + bglu_ref[...])
    mix = _dot(ya_ref[...], wa_ref[...]) + _dot(yb.astype(BF16), wb_ref[...])
    o_ref[...] = _layer_norm(ALPHA * x_ref[...] + mix, g_ref[...], b_ref[...])


def _ab_out(x, ya, ys, us, d, wglu, bglu, wa, wb, g, b):
    n = x.shape[0]

    def rows(width):
        return pl.BlockSpec((ROW_TILE, width), lambda i: (i, 0))

    consts = (d, wglu, bglu, wa, wb, g, b)
    return pl.pallas_call(
        _ab_out_kernel,
        grid=(n // ROW_TILE,),
        in_specs=[rows(D_MODEL), rows(ATT_WIDTH), rows(SSM_WIDTH), rows(SSM_WIDTH)] + [_resident(c.shape) for c in consts],
        out_specs=rows(D_MODEL),
        out_shape=jax.ShapeDtypeStruct((n, D_MODEL), F32),
        compiler_params=_params(1),
        name="ab_out_ln",
    )(x, ya, ys, us, *consts)


def _cd_kernel(seq, x_ref, halo_ref, win_ref, pw_ref, ps_ref, sg_ref, sb_ref, ws_ref, bs_ref,
               wout_ref, g_ref, b_ref, o_ref):
    x = x_ref[...]
    h = _dot(x.astype(BF16), win_ref[...])
    xc = h[:, :POOL_WIDTH]
    row0 = (pl.program_id(0) * ROW_TILE) % seq
    hc = _dot(halo_ref[...].astype(BF16), win_ref[:, :POOL_WIDTH])
    hc = jnp.where(row0 == 0, 0.0, hc)
    ext = jnp.concatenate([hc, xc], axis=0)
    pos = row0 + lax.broadcasted_iota(jnp.int32, (ROW_TILE, POOL_GROUP), 0)
    acc = ext
    ys = []
    for gi, w in enumerate(POOL_WINDOWS):
        acc = acc + pltpu.roll(acc, w // 2, 0)
        cnt = jnp.minimum(pos + 1, w).astype(F32)
        pooled = acc[POOL_HALO:, :POOL_GROUP] / cnt - xc[:, gi * POOL_GROUP:(gi + 1) * POOL_GROUP]
        ys.append(_dot(pooled.astype(BF16), pw_ref[gi]))
        if gi + 1 < len(POOL_WINDOWS):
            acc = acc[:, POOL_GROUP:]
    yc = jnp.concatenate(ys, axis=-1) * ps_ref[...]

    z = _gelu_tanh(h[:, POOL_WIDTH:])
    u = z[:, :SGU_WIDTH]
    v = _layer_norm(z[:, SGU_WIDTH:], sg_ref[...], sb_ref[...]).astype(BF16)
    hd = SGU_WIDTH // SGU_HEADS
    blocks = []
    for r in range(ROW_TILE // SGU_CHUNK):
        rs = slice(r * SGU_CHUNK, (r + 1) * SGU_CHUNK)
        heads = [_dot(ws_ref[hh], v[rs, hh * hd:(hh + 1) * hd]) for hh in range(SGU_HEADS)]
        blocks.append(jnp.concatenate(heads, axis=-1) + bs_ref[...])
    yd = u * jnp.concatenate(blocks, axis=0)

    mix = _dot(yc.astype(BF16), wout_ref[:POOL_WIDTH, :]) + _dot(yd.astype(BF16), wout_ref[POOL_WIDTH:, :])
    o_ref[...] = _layer_norm(ALPHA * x + mix, g_ref[...], b_ref[...])


def _cd_mixer(x, seq, win, pw, ps, sg, sb, ws, bs, wout, g, b):
    n = x.shape[0]
    halo_per_tile = ROW_TILE // POOL_HALO
    row = pl.BlockSpec((ROW_TILE, D_MODEL), lambda i: (i, 0))
    halo = pl.BlockSpec((POOL_HALO, D_MODEL), lambda i: (jnp.maximum(i * halo_per_tile - 1, 0), 0))
    consts = (win, pw, ps, sg, sb, ws, bs, wout, g, b)
    return pl.pallas_call(
        functools.partial(_cd_kernel, seq),
        grid=(n // ROW_TILE,),
        in_specs=[row, halo] + [_resident(c.shape) for c in consts],
        out_specs=row,
        out_shape=jax.ShapeDtypeStruct((n, D_MODEL), F32),
        compiler_params=_params(1),
        name="cd_mixer_ln",
    )(x, x, *consts)


def kernel(x, ln_g, ln_b, ffn_w_gate, ffn_w_up, ffn_w_down, ab_w_in, ab_w_out, att_rel_bias, ssm_a_re, ssm_a_im, ssm_log_dt, ssm_b_re, ssm_b_im, ssm_c_re, ssm_c_im, ssm_d, ssm_w_glu, ssm_b_glu, cd_w_in, cd_w_out, pool_w, pool_scale, sgu_ln_g, sgu_ln_b, sgu_w_s, sgu_b_s):
    bsz, seq, d_model = x.shape
    assert d_model == D_MODEL and seq % ROW_TILE == 0 and seq % Q_TILE == 0 and seq % SSM_T == 0
    assert bsz % min(SSM_SEQS, bsz) == 0 and ln_g.shape[0] == DEPTH
    n = bsz * seq
    h = x.reshape(n, D_MODEL).astype(F32)

    def ln_params(l, j):
        return ln_g[l, j].astype(F32).reshape(1, D_MODEL), ln_b[l, j].astype(F32).reshape(1, D_MODEL)

    def ffn(h, l, j):
        return _ffn_ln(h, ffn_w_gate[l, j].astype(BF16), ffn_w_up[l, j].astype(BF16),
                       ffn_w_down[l, j].astype(BF16), *ln_params(l, j if j == 0 else 2))

    q_scale = jnp.concatenate([jnp.full((ATT_WIDTH,), ATT_HEAD_DIM ** -0.5 * LOG2_E, F32),
                               jnp.ones((2 * ATT_WIDTH + SSM_WIDTH,), F32)])
    tril = jnp.tril(jnp.ones((SGU_CHUNK, SGU_CHUNK), F32))

    for l in range(DEPTH):
        h = ffn(h, l, 0)
        i = l // 2
        if l % 2 == 0:
            qkv, us, us_b = _ab_in(h, (ab_w_in[i].astype(F32) * q_scale).astype(BF16))
            ya = _attention(qkv, _attention_bias(att_rel_bias[i]), bsz, seq)
            mats = _ssm_matrices(ssm_a_re[i], ssm_a_im[i], ssm_log_dt[i], ssm_b_re[i], ssm_b_im[i],
                                 ssm_c_re[i], ssm_c_im[i], seq // SSM_T)
            ys = _ssm(us_b, mats, bsz, seq)
            w_out = ab_w_out[i].astype(BF16)
            h = _ab_out(h, ya, ys, us, ssm_d[i].astype(F32).reshape(1, SSM_WIDTH), ssm_w_glu[i].astype(BF16),
                        ssm_b_glu[i].astype(F32).reshape(1, SSM_WIDTH), w_out[:ATT_WIDTH], w_out[ATT_WIDTH:],
                        *ln_params(l, 1))
        else:
            bs_tbl = jnp.repeat(sgu_b_s[i].astype(F32).T, SGU_WIDTH // SGU_HEADS, axis=1)
            h = _cd_mixer(h, seq, cd_w_in[i].astype(BF16), pool_w[i].astype(BF16),
                          pool_scale[i].astype(F32).reshape(1, POOL_WIDTH),
                          sgu_ln_g[i].astype(F32).reshape(1, SGU_WIDTH), sgu_ln_b[i].astype(F32).reshape(1, SGU_WIDTH),
                          (sgu_w_s[i].astype(F32) * tril).astype(BF16), bs_tbl,
                          cd_w_out[i].astype(BF16), *ln_params(l, 1))
        h = ffn(h, l, 1)
    return h.reshape(bsz, seq, D_MODEL).astype(x.dtype)
```

```python
import functools

import jax
import jax.numpy as jnp
from jax import lax
from jax.experimental import pallas as pl
from jax.experimental.pallas import tpu as pltpu

BF16 = jnp.bfloat16
F32 = jnp.float32

DEPTH = 4
D_MODEL = 1024
D_FF = 2816
ALPHA = (2 * DEPTH) ** 0.25
LN_EPS = 1e-5
NEG_INF = -1e30
LOG2_E = 1.4426950408889634

CHUNK = 64
N_PREV_CHUNKS = 8
REL_CLIP = 128
ATT_HEAD_DIM = 64
ATT_WIDTH = 768
ATT_HEADS = 12
SSM_WIDTH = 256
SSM_GROUP = 16
SSM_GROUPS = 16
SSM_STATE = 64
POOL_WINDOWS = (2, 4, 8, 16)
POOL_WIDTH = 512
POOL_GROUP = 128
SGU_WIDTH = 512
SGU_CHUNK = 128
SGU_HEADS = 4

V7X_LANES = 128
V7X_MXU_DIM = 256
V7X_VMEM_LIMIT_BYTES = 60000 * 1024

ROW_TILE = 512
FF_CHUNK = V7X_MXU_DIM
Q_TILE = 4 * CHUNK
K_WINDOW = Q_TILE + N_PREV_CHUNKS * CHUNK
SSM_T = 8
SSM_SEQS = 2
POOL_HALO = 16


def _params(n_axes):
    return pltpu.CompilerParams(
        dimension_semantics=("arbitrary",) * n_axes,
        vmem_limit_bytes=V7X_VMEM_LIMIT_BYTES)


def _resident(shape):
    nd = len(shape)
    return pl.BlockSpec(shape, lambda *_: (0,) * nd, pipeline_mode=pl.Buffered(1))


def _layer_norm(z, g, b):
    mu = jnp.mean(z, axis=-1, keepdims=True)
    zc = z - mu
    var = jnp.mean(zc * zc, axis=-1, keepdims=True)
    return zc * lax.rsqrt(var + LN_EPS) * g + b


def _gelu_tanh(x):
    return 0.5 * x * (1.0 + jnp.tanh(0.7978845608028654 * (x + 0.044715 * (x * x * x))))


def _dot(a, b):
    return jnp.dot(a, b, preferred_element_type=F32)


def _ffn_kernel(x_ref, wg_ref, wu_ref, wd_ref, g_ref, b_ref, o_ref, h_scr):
    x = x_ref[...]
    xb = x.astype(BF16)
    for c in range(D_FF // FF_CHUNK):
        sl = slice(c * FF_CHUNK, (c + 1) * FF_CHUNK)
        gate = _dot(xb, wg_ref[:, sl])
        up = _dot(xb, wu_ref[:, sl])
        h_scr[:, sl] = (gate * jax.nn.sigmoid(gate) * up).astype(BF16)
    y = _dot(h_scr[...], wd_ref[...])
    o_ref[...] = _layer_norm(ALPHA * x + 0.5 * y, g_ref[...], b_ref[...])


def _ffn_ln(x, wg, wu, wd, layer, pos, g, b):
    n = x.shape[0]
    row = pl.BlockSpec((ROW_TILE, D_MODEL), lambda i: (i, 0))

    def picked(rows, cols):
        return pl.BlockSpec((None, None, rows, cols), lambda i: (layer, pos, 0, 0), pipeline_mode=pl.Buffered(1))

    return pl.pallas_call(
        _ffn_kernel,
        grid=(n // ROW_TILE,),
        in_specs=[row, picked(D_MODEL, D_FF), picked(D_MODEL, D_FF), picked(D_FF, D_MODEL),
                  _resident((1, D_MODEL)), _resident((1, D_MODEL))],
        out_specs=row,
        out_shape=jax.ShapeDtypeStruct((n, D_MODEL), F32),
        scratch_shapes=[pltpu.VMEM((ROW_TILE, D_FF), BF16)],
        compiler_params=_params(1),
        name="ffn_ln",
    )(x, wg, wu, wd, g, b)


def _ab_in_kernel(x_ref, w_ref, qkv_ref, us_ref, usb_ref):
    h = _dot(x_ref[...].astype(BF16), w_ref[...])
    qkv_ref[...] = h[:, :3 * ATT_WIDTH].astype(BF16)
    us = h[:, 3 * ATT_WIDTH:]
    us_ref[...] = us
    usb_ref[...] = us.astype(BF16)


def _ab_in(x, w):
    n = x.shape[0]
    width = 3 * ATT_WIDTH + SSM_WIDTH
    return pl.pallas_call(
        _ab_in_kernel,
        grid=(n // ROW_TILE,),
        in_specs=[pl.BlockSpec((ROW_TILE, D_MODEL), lambda i: (i, 0)), _resident((D_MODEL, width))],
        out_specs=[pl.BlockSpec((ROW_TILE, 3 * ATT_WIDTH), lambda i: (i, 0)),
                   pl.BlockSpec((ROW_TILE, SSM_WIDTH), lambda i: (i, 0)),
                   pl.BlockSpec((ROW_TILE, SSM_WIDTH), lambda i: (i, 0))],
        out_shape=[jax.ShapeDtypeStruct((n, 3 * ATT_WIDTH), BF16),
                   jax.ShapeDtypeStruct((n, SSM_WIDTH), F32),
                   jax.ShapeDtypeStruct((n, SSM_WIDTH), BF16)],
        compiler_params=_params(1),
        name="ab_in",
    )(x, w)


def _attn_kernel(q_ref, k_ref, v_ref, bias_ref, o_ref):
    seq = q_ref.shape[0]
    lane = lax.broadcasted_iota(jnp.int32, (Q_TILE, V7X_LANES), 1)
    first_head = lane < ATT_HEAD_DIM
    for t in range(seq // Q_TILE):
        q0 = t * Q_TILE
        k0 = max(0, q0 + Q_TILE - K_WINDOW)
        width = q0 + Q_TILE - k0
        q = q_ref[q0:q0 + Q_TILE, :]
        k = k_ref[k0:k0 + width, :]
        v = v_ref[k0:k0 + width, :]
        zero = jnp.zeros_like(q)
        qs = jnp.concatenate([jnp.where(first_head, q, zero), jnp.where(first_head, zero, q)], axis=0)
        s = lax.dot_general(qs, k, (((1,), (1,)), ((), ())), preferred_element_type=F32)
        s = s + bias_ref[:, :, K_WINDOW - width:].reshape(2 * Q_TILE, width)
        p = jnp.exp2(s - jnp.max(s, axis=-1, keepdims=True))
        o = _dot(p.astype(BF16), v) / jnp.sum(p, axis=-1, keepdims=True)
        o_ref[q0:q0 + Q_TILE, :] = jnp.where(first_head, o[:Q_TILE], o[Q_TILE:]).astype(BF16)


def _attention(qkv, bias, bsz, seq):
    pairs = ATT_WIDTH // V7X_LANES
    qkv3 = qkv.reshape(bsz, seq, 3 * ATT_WIDTH)

    def col_block(offset):
        return pl.BlockSpec((None, seq, V7X_LANES), lambda b, j: (b, 0, offset + j))

    out = pl.pallas_call(
        _attn_kernel,
        grid=(bsz, pairs),
        in_specs=[col_block(0), col_block(pairs), col_block(2 * pairs),
                  pl.BlockSpec((2, Q_TILE, K_WINDOW), lambda b, j: (j, 0, 0))],
        out_specs=pl.BlockSpec((None, seq, V7X_LANES), lambda b, j: (b, 0, j)),
        out_shape=jax.ShapeDtypeStruct((bsz, seq, ATT_WIDTH), BF16),
        compiler_params=_params(2),
        name="band_attention",
    )(qkv3, qkv3, qkv3, bias)
    return out.reshape(bsz * seq, ATT_WIDTH)


def _attention_bias(rel_bias):
    heads = rel_bias.shape[0]
    rb = rel_bias.astype(F32) * LOG2_E
    d_min = N_PREV_CHUNKS * CHUNK - (K_WINDOW - 1)
    d_max = N_PREV_CHUNKS * CHUNK + Q_TILE - 1
    n = d_max - d_min + 1
    by_dist = jnp.concatenate([jnp.repeat(rb[:, :1], -REL_CLIP - d_min, axis=1), rb,
                               jnp.repeat(rb[:, -1:], d_max - REL_CLIP, axis=1)], axis=1)
    rev = jnp.concatenate([by_dist[:, ::-1], jnp.zeros((heads, 1), F32)], axis=1)
    skew = jnp.tile(rev, (1, Q_TILE))[:, :Q_TILE * n].reshape(heads, Q_TILE, n)
    table = skew[:, :, Q_TILE - 1:Q_TILE - 1 + K_WINDOW]
    qc = jnp.arange(Q_TILE)[:, None] // CHUNK
    kc = jnp.arange(K_WINDOW)[None, :] // CHUNK
    in_band = (kc >= qc) & (kc <= qc + N_PREV_CHUNKS)
    return jnp.where(in_band[None], table, NEG_INF)


def _cmul(ar, ai, br, bi):
    return ar * br - ai * bi, ar * bi + ai * br


def _ssm_kernel(n_chunk, x_ref, mloc_ref, we_ref, wc_ref, mu_ref, o_ref):
    rows = x_ref.shape[0]
    half = SSM_GROUPS * SSM_STATE
    xb = x_ref[...]
    e = _dot(xb, we_ref[...])
    er, ei = e[:, :half], e[:, half:]
    chunk_id = lax.broadcasted_iota(jnp.int32, (rows, half), 0) % n_chunk
    step = 1
    k = 0
    while step < n_chunk:
        pr = jnp.where(chunk_id >= step, pltpu.roll(er, step, 0), 0.0)
        pi = jnp.where(chunk_id >= step, pltpu.roll(ei, step, 0), 0.0)
        pr, pi = _cmul(pr, pi, mu_ref[2 * k:2 * k + 1, :], mu_ref[2 * k + 1:2 * k + 2, :])
        er, ei = er + pr, ei + pi
        step *= 2
        k += 1
    sr = jnp.where(chunk_id >= 1, pltpu.roll(er, 1, 0), 0.0)
    si = jnp.where(chunk_id >= 1, pltpu.roll(ei, 1, 0), 0.0)
    state = jnp.concatenate([sr, si], axis=-1).astype(BF16)
    o_ref[...] = _dot(xb, mloc_ref[...]) + _dot(state, wc_ref[...])


def _ssm(us, mats, bsz, seq):
    mloc, we, wc, mu = mats
    n_chunk = seq // SSM_T
    n_seq = min(SSM_SEQS, bsz)
    rows = n_seq * n_chunk
    assert us.dtype == BF16
    x = us.reshape(bsz * n_chunk, SSM_T * SSM_WIDTH)
    blk = pl.BlockSpec((rows, SSM_T * SSM_WIDTH), lambda i: (i, 0))
    out = pl.pallas_call(
        functools.partial(_ssm_kernel, n_chunk),
        grid=(bsz // n_seq,),
        in_specs=[blk] + [_resident(m.shape) for m in mats],
        out_specs=blk,
        out_shape=jax.ShapeDtypeStruct(x.shape, F32),
        compiler_params=_params(1),
        name="s5_scan",
    )(x, mloc, we, wc, mu)
    return out.reshape(bsz * seq, SSM_WIDTH)


def _ssm_matrices(a_re, a_im, log_dt, b_re, b_im, c_re, c_im, n_chunk):
    hi = lax.Precision.HIGHEST
    dt = jnp.exp(log_dt.astype(F32))[:, None]
    a_re = a_re.astype(F32)
    a_im = a_im.astype(F32)
    mag = jnp.exp(dt * a_re)
    ang = dt * a_im
    ab_re = mag * jnp.cos(ang)
    ab_im = mag * jnp.sin(ang)
    den = a_re * a_re + a_im * a_im
    nr = ab_re - 1.0
    ni = ab_im
    coef_re = (nr * a_re + ni * a_im) / den
    coef_im = (ni * a_re - nr * a_im) / den
    b_re = b_re.astype(F32)
    b_im = b_im.astype(F32)
    bb_re = coef_re[..., None] * b_re - coef_im[..., None] * b_im
    bb_im = coef_re[..., None] * b_im + coef_im[..., None] * b_re
    c_re = c_re.astype(F32)
    c_im = c_im.astype(F32)
    pw_r, pw_i = [jnp.ones_like(ab_re)], [jnp.zeros_like(ab_im)]
    for _ in range(SSM_T):
        r, i = _cmul(pw_r[-1], pw_i[-1], ab_re, ab_im)
        pw_r.append(r)
        pw_i.append(i)
    pw_r, pw_i = jnp.stack(pw_r), jnp.stack(pw_i)
    half = SSM_GROUPS * SSM_STATE
    width = SSM_T * SSM_WIDTH

    def spread(m):
        tiled = jnp.tile(m, (1, 1, SSM_GROUPS))
        row_g = jnp.arange(m.shape[1]) // (m.shape[1] // SSM_GROUPS)
        col_g = jnp.arange(tiled.shape[2]) // m.shape[2]
        return jnp.where(row_g[:, None] == col_g[None, :], tiled, 0.0)

    cp_r = c_re[None] * pw_r[:, :, None, :] - c_im[None] * pw_i[:, :, None, :]
    cp_i = c_re[None] * pw_i[:, :, None, :] + c_im[None] * pw_r[:, :, None, :]
    kern = (jnp.einsum('tghp,gpk->tgkh', cp_r[:SSM_T], bb_re, precision=hi)
            - jnp.einsum('tghp,gpk->tgkh', cp_i[:SSM_T], bb_im, precision=hi))
    lag_blocks = spread(kern.reshape(SSM_T, SSM_WIDTH, SSM_GROUP))
    no_block = jnp.zeros((SSM_WIDTH, SSM_WIDTH), F32)
    mloc = jnp.block([[lag_blocks[t - s] if t >= s else no_block for t in range(SSM_T)]
                      for s in range(SSM_T)])

    rev_r, rev_i = pw_r[SSM_T - 1::-1], pw_i[SSM_T - 1::-1]
    pb_r = rev_r[..., None] * bb_re[None] - rev_i[..., None] * bb_im[None]
    pb_i = rev_r[..., None] * bb_im[None] + rev_i[..., None] * bb_re[None]

    def drive(m):
        m = jnp.transpose(m, (0, 1, 3, 2)).reshape(SSM_T, SSM_WIDTH, SSM_STATE)
        return spread(m).reshape(width, half)

    we = jnp.concatenate([drive(pb_r), drive(pb_i)], axis=1)

    def read(m):
        m = jnp.transpose(m[1:], (0, 1, 3, 2)).reshape(SSM_T, half, SSM_GROUP)
        return jnp.concatenate(list(spread(m)), axis=1)

    wc = jnp.concatenate([read(cp_r), -read(cp_i)], axis=0)

    pr, pi = pw_r[SSM_T].reshape(1, half), pw_i[SSM_T].reshape(1, half)
    mus = []
    step = 1
    while step < n_chunk:
        mus += [pr, pi]
        pr, pi = _cmul(pr, pi, pr, pi)
        step *= 2
    return mloc.astype(BF16), we.astype(BF16), wc.astype(BF16), jnp.concatenate(mus, axis=0)


def _ab_out_kernel(x_ref, ya_ref, ys_ref, us_ref, d_ref, wglu_ref, bglu_ref, wa_ref, wb_ref, g_ref, b_ref, o_ref):
    y = ys_ref[...] + d_ref[...] * us_ref[...]
    gl = _gelu_tanh(y)
    yb = gl * jax.nn.sigmoid(_dot(gl.astype(BF16), wglu_ref[...]) + bglu_ref[...])
    mix = _dot(ya_ref[...], wa_ref[...]) + _dot(yb.astype(BF16), wb_ref[...])
    o_ref[...] = _layer_norm(ALPHA * x_ref[...] + mix, g_ref[...], b_ref[...])


def _ab_out(x, ya, ys, us, d, wglu, bglu, wa, wb, g, b):
    n = x.shape[0]

    def rows(width):
        return pl.BlockSpec((ROW_TILE, width), lambda i: (i, 0))

    consts = (d, wglu, bglu, wa, wb, g, b)
    return pl.pallas_call(
        _ab_out_kernel,
        grid=(n // ROW_TILE,),
        in_specs=[rows(D_MODEL), rows(ATT_WIDTH), rows(SSM_WIDTH), rows(SSM_WIDTH)] + [_resident(c.shape) for c in consts],
        out_specs=rows(D_MODEL),
        out_shape=jax.ShapeDtypeStruct((n, D_MODEL), F32),
        compiler_params=_params(1),
        name="ab_out_ln",
    )(x, ya, ys, us, *consts)


def _cd_kernel(seq, x_ref, halo_ref, win_ref, pw_ref, ps_ref, sg_ref, sb_ref, ws_ref, bs_ref,
               wout_ref, g_ref, b_ref, o_ref):
    x = x_ref[...]
    h = _dot(x.astype(BF16), win_ref[...])
    xc = h[:, :POOL_WIDTH]
    row0 = (pl.program_id(0) * ROW_TILE) % seq
    hc = _dot(halo_ref[...].astype(BF16), win_ref[:, :POOL_WIDTH])
    hc = jnp.where(row0 == 0, 0.0, hc)
    ext = jnp.concatenate([hc, xc], axis=0)
    pos = row0 + lax.broadcasted_iota(jnp.int32, (ROW_TILE, POOL_GROUP), 0)
    acc = ext
    ys = []
    for gi, w in enumerate(POOL_WINDOWS):
        acc = acc + pltpu.roll(acc, w // 2, 0)
        cnt = jnp.minimum(pos + 1, w).astype(F32)
        pooled = acc[POOL_HALO:, :POOL_GROUP] / cnt - xc[:, gi * POOL_GROUP:(gi + 1) * POOL_GROUP]
        ys.append(_dot(pooled.astype(BF16), pw_ref[gi]))
        if gi + 1 < len(POOL_WINDOWS):
            acc = acc[:, POOL_GROUP:]
    yc = jnp.concatenate(ys, axis=-1) * ps_ref[...]

    z = _gelu_tanh(h[:, POOL_WIDTH:])
    u = z[:, :SGU_WIDTH]
    v = _layer_norm(z[:, SGU_WIDTH:], sg_ref[...], sb_ref[...]).astype(BF16)
    hd = SGU_WIDTH // SGU_HEADS
    blocks = []
    for r in range(ROW_TILE // SGU_CHUNK):
        rs = slice(r * SGU_CHUNK, (r + 1) * SGU_CHUNK)
        heads = [_dot(ws_ref[hh], v[rs, hh * hd:(hh + 1) * hd]) for hh in range(SGU_HEADS)]
        blocks.append(jnp.concatenate(heads, axis=-1) + bs_ref[...])
    yd = u * jnp.concatenate(blocks, axis=0)

    mix = _dot(yc.astype(BF16), wout_ref[:POOL_WIDTH, :]) + _dot(yd.astype(BF16), wout_ref[POOL_WIDTH:, :])
    o_ref[...] = _layer_norm(ALPHA * x + mix, g_ref[...], b_ref[...])


def _cd_mixer(x, seq, win, pw, ps, sg, sb, ws, bs, wout, g, b):
    n = x.shape[0]
    halo_per_tile = ROW_TILE // POOL_HALO
    row = pl.BlockSpec((ROW_TILE, D_MODEL), lambda i: (i, 0))
    halo = pl.BlockSpec((POOL_HALO, D_MODEL), lambda i: (jnp.maximum(i * halo_per_tile - 1, 0), 0))
    consts = (win, pw, ps, sg, sb, ws, bs, wout, g, b)
    return pl.pallas_call(
        functools.partial(_cd_kernel, seq),
        grid=(n // ROW_TILE,),
        in_specs=[row, halo] + [_resident(c.shape) for c in consts],
        out_specs=row,
        out_shape=jax.ShapeDtypeStruct((n, D_MODEL), F32),
        compiler_params=_params(1),
        name="cd_mixer_ln",
    )(x, x, *consts)


def kernel(x, ln_g, ln_b, ffn_w_gate, ffn_w_up, ffn_w_down, ab_w_in, ab_w_out, att_rel_bias, ssm_a_re, ssm_a_im, ssm_log_dt, ssm_b_re, ssm_b_im, ssm_c_re, ssm_c_im, ssm_d, ssm_w_glu, ssm_b_glu, cd_w_in, cd_w_out, pool_w, pool_scale, sgu_ln_g, sgu_ln_b, sgu_w_s, sgu_b_s):
    bsz, seq, d_model = x.shape
    assert d_model == D_MODEL and seq % ROW_TILE == 0 and seq % Q_TILE == 0 and seq % SSM_T == 0
    assert bsz % min(SSM_SEQS, bsz) == 0 and ln_g.shape[0] == DEPTH
    n = bsz * seq
    h = x.reshape(n, D_MODEL).astype(F32)

    def ln_params(l, j):
        return ln_g[l, j].astype(F32).reshape(1, D_MODEL), ln_b[l, j].astype(F32).reshape(1, D_MODEL)

    wg_all, wu_all, wd_all = ffn_w_gate.astype(BF16), ffn_w_up.astype(BF16), ffn_w_down.astype(BF16)

    def ffn(h, l, j):
        return _ffn_ln(h, wg_all, wu_all, wd_all, l, j, *ln_params(l, j if j == 0 else 2))

    q_scale = jnp.concatenate([jnp.full((ATT_WIDTH,), ATT_HEAD_DIM ** -0.5 * LOG2_E, F32),
                               jnp.ones((2 * ATT_WIDTH + SSM_WIDTH,), F32)])
    tril = jnp.tril(jnp.ones((SGU_CHUNK, SGU_CHUNK), F32))

    for l in range(DEPTH):
        h = ffn(h, l, 0)
        i = l // 2
        if l % 2 == 0:
            qkv, us, us_b = _ab_in(h, (ab_w_in[i].astype(F32) * q_scale).astype(BF16))
            ya = _attention(qkv, _attention_bias(att_rel_bias[i]), bsz, seq)
            mats = _ssm_matrices(ssm_a_re[i], ssm_a_im[i], ssm_log_dt[i], ssm_b_re[i], ssm_b_im[i],
                                 ssm_c_re[i], ssm_c_im[i], seq // SSM_T)
            ys = _ssm(us_b, mats, bsz, seq)
            w_out = ab_w_out[i].astype(BF16)
            h = _ab_out(h, ya, ys, us, ssm_d[i].astype(F32).reshape(1, SSM_WIDTH), ssm_w_glu[i].astype(BF16),
                        ssm_b_glu[i].astype(F32).reshape(1, SSM_WIDTH), w_out[:ATT_WIDTH], w_out[ATT_WIDTH:],
                        *ln_params(l, 1))
        else:
            bs_tbl = jnp.repeat(sgu_b_s[i].astype(F32).T, SGU_WIDTH // SGU_HEADS, axis=1)
            h = _cd_mixer(h, seq, cd_w_in[i].astype(BF16), pool_w[i].astype(BF16),
                          pool_scale[i].astype(F32).reshape(1, POOL_WIDTH),
                          sgu_ln_g[i].astype(F32).reshape(1, SGU_WIDTH), sgu_ln_b[i].astype(F32).reshape(1, SGU_WIDTH),
                          (sgu_w_s[i].astype(F32) * tril).astype(BF16), bs_tbl,
                          cd_w_out[i].astype(BF16), *ln_params(l, 1))
        h = ffn(h, l, 1)
    return h.reshape(bsz, seq, D_MODEL).astype(x.dtype)
```

```python
import functools

import jax
import jax.numpy as jnp
from jax import lax
from jax.experimental import pallas as pl
from jax.experimental.pallas import tpu as pltpu

BF16 = jnp.bfloat16
F32 = jnp.float32

DEPTH = 4
D_MODEL = 1024
D_FF = 2816
ALPHA = (2 * DEPTH) ** 0.25
LN_EPS = 1e-5
NEG_INF = -1e30
LOG2_E = 1.4426950408889634

CHUNK = 64
N_PREV_CHUNKS = 8
REL_CLIP = 128
ATT_HEAD_DIM = 64
ATT_WIDTH = 768
ATT_HEADS = 12
SSM_WIDTH = 256
SSM_GROUP = 16
SSM_GROUPS = 16
SSM_STATE = 64
POOL_WINDOWS = (2, 4, 8, 16)
POOL_WIDTH = 512
POOL_GROUP = 128
SGU_WIDTH = 512
SGU_CHUNK = 128
SGU_HEADS = 4

V7X_LANES = 128
V7X_MXU_DIM = 256
V7X_VMEM_LIMIT_BYTES = 60000 * 1024

ROW_TILE = 512
FF_CHUNK = V7X_MXU_DIM
Q_TILE = 4 * CHUNK
K_WINDOW = Q_TILE + N_PREV_CHUNKS * CHUNK
SSM_T = 8
SSM_SEQS = 2
SSM_SLABS = SSM_WIDTH // V7X_LANES
POOL_HALO = 16


def _params(n_axes):
    return pltpu.CompilerParams(
        dimension_semantics=("arbitrary",) * n_axes,
        vmem_limit_bytes=V7X_VMEM_LIMIT_BYTES)


def _resident(shape):
    nd = len(shape)
    return pl.BlockSpec(shape, lambda *_: (0,) * nd, pipeline_mode=pl.Buffered(1))


def _layer_norm(z, g, b):
    mu = jnp.mean(z, axis=-1, keepdims=True)
    zc = z - mu
    var = jnp.mean(zc * zc, axis=-1, keepdims=True)
    return zc * lax.rsqrt(var + LN_EPS) * g + b


def _gelu_tanh(x):
    return 0.5 * x * (1.0 + jnp.tanh(0.7978845608028654 * (x + 0.044715 * (x * x * x))))


def _dot(a, b):
    return jnp.dot(a, b, preferred_element_type=F32)


def _ffn_kernel(n_tiles, x_ref, wg_ref, wu_ref, wd_ref, g_ref, b_ref, o_ref, h_scr, x_scr, y_scr):
    i = pl.program_id(0)

    def branch_into_scratch():
        x = x_ref[...]
        xb = x.astype(BF16)
        for c in range(D_FF // FF_CHUNK):
            sl = slice(c * FF_CHUNK, (c + 1) * FF_CHUNK)
            gate = _dot(xb, wg_ref[:, sl])
            up = _dot(xb, wu_ref[:, sl])
            h_scr[:, sl] = (gate * jax.nn.sigmoid(gate) * up).astype(BF16)
        y = _dot(h_scr[...], wd_ref[...])
        x_scr[...] = x
        y_scr[...] = y

    def finish_previous():
        o_ref[...] = _layer_norm(ALPHA * x_scr[...] + 0.5 * y_scr[...], g_ref[...], b_ref[...])

    @pl.when(i == 0)
    def _():
        branch_into_scratch()

    @pl.when(jnp.logical_and(i > 0, i < n_tiles))
    def _():
        finish_previous()
        branch_into_scratch()

    @pl.when(i == n_tiles)
    def _():
        finish_previous()


def _ffn_ln(x, wg, wu, wd, layer, pos, g, b):
    n = x.shape[0]
    n_tiles = n // ROW_TILE

    def picked(rows, cols):
        return pl.BlockSpec((None, None, rows, cols), lambda i: (layer, pos, 0, 0), pipeline_mode=pl.Buffered(1))

    return pl.pallas_call(
        functools.partial(_ffn_kernel, n_tiles),
        grid=(n_tiles + 1,),
        in_specs=[pl.BlockSpec((ROW_TILE, D_MODEL), lambda i: (jnp.minimum(i, n_tiles - 1), 0)),
                  picked(D_MODEL, D_FF), picked(D_MODEL, D_FF), picked(D_FF, D_MODEL),
                  _resident((1, D_MODEL)), _resident((1, D_MODEL))],
        out_specs=pl.BlockSpec((ROW_TILE, D_MODEL), lambda i: (jnp.maximum(i - 1, 0), 0)),
        out_shape=jax.ShapeDtypeStruct((n, D_MODEL), F32),
        scratch_shapes=[pltpu.VMEM((ROW_TILE, D_FF), BF16), pltpu.VMEM((ROW_TILE, D_MODEL), F32),
                        pltpu.VMEM((ROW_TILE, D_MODEL), F32)],
        compiler_params=_params(1),
        name="ffn_ln",
    )(x, wg, wu, wd, g, b)


def _ab_in_kernel(x_ref, w_ref, qkv_ref, us_ref):
    h = _dot(x_ref[...].astype(BF16), w_ref[...])
    qkv_ref[...] = h[:, :3 * ATT_WIDTH].astype(BF16)
    for q in range(SSM_SLABS):
        lo = 3 * ATT_WIDTH + q * V7X_LANES
        us_ref[q] = h[:, lo:lo + V7X_LANES]


def _ab_in(x, w):
    n = x.shape[0]
    width = 3 * ATT_WIDTH + SSM_WIDTH
    return pl.pallas_call(
        _ab_in_kernel,
        grid=(n // ROW_TILE,),
        in_specs=[pl.BlockSpec((ROW_TILE, D_MODEL), lambda i: (i, 0)), _resident((D_MODEL, width))],
        out_specs=[pl.BlockSpec((ROW_TILE, 3 * ATT_WIDTH), lambda i: (i, 0)),
                   pl.BlockSpec((SSM_SLABS, ROW_TILE, V7X_LANES), lambda i: (0, i, 0))],
        out_shape=[jax.ShapeDtypeStruct((n, 3 * ATT_WIDTH), BF16),
                   jax.ShapeDtypeStruct((SSM_SLABS, n, V7X_LANES), F32)],
        compiler_params=_params(1),
        name="ab_in",
    )(x, w)


def _attn_kernel(q_ref, k_ref, v_ref, bias_ref, o_ref):
    seq = q_ref.shape[0]
    lane = lax.broadcasted_iota(jnp.int32, (Q_TILE, V7X_LANES), 1)
    first_head = lane < ATT_HEAD_DIM
    for t in range(seq // Q_TILE):
        q0 = t * Q_TILE
        k0 = max(0, q0 + Q_TILE - K_WINDOW)
        width = q0 + Q_TILE - k0
        q = q_ref[q0:q0 + Q_TILE, :]
        k = k_ref[k0:k0 + width, :]
        v = v_ref[k0:k0 + width, :]
        zero = jnp.zeros_like(q)
        qs = jnp.concatenate([jnp.where(first_head, q, zero), jnp.where(first_head, zero, q)], axis=0)
        s = lax.dot_general(qs, k, (((1,), (1,)), ((), ())), preferred_element_type=F32)
        s = s + bias_ref[:, :, K_WINDOW - width:].reshape(2 * Q_TILE, width)
        p = jnp.exp2(s - jnp.max(s, axis=-1, keepdims=True))
        o = _dot(p.astype(BF16), v) / jnp.sum(p, axis=-1, keepdims=True)
        o_ref[q0:q0 + Q_TILE, :] = jnp.where(first_head, o[:Q_TILE], o[Q_TILE:]).astype(BF16)


def _attention(qkv, bias, bsz, seq):
    pairs = ATT_WIDTH // V7X_LANES
    qkv3 = qkv.reshape(bsz, seq, 3 * ATT_WIDTH)

    def col_block(offset):
        return pl.BlockSpec((None, seq, V7X_LANES), lambda b, j: (b, 0, offset + j))

    out = pl.pallas_call(
        _attn_kernel,
        grid=(bsz, pairs),
        in_specs=[col_block(0), col_block(pairs), col_block(2 * pairs),
                  pl.BlockSpec((2, Q_TILE, K_WINDOW), lambda b, j: (j, 0, 0))],
        out_specs=pl.BlockSpec((None, seq, V7X_LANES), lambda b, j: (b, 0, j)),
        out_shape=jax.ShapeDtypeStruct((bsz, seq, ATT_WIDTH), BF16),
        compiler_params=_params(2),
        name="band_attention",
    )(qkv3, qkv3, qkv3, bias)
    return out.reshape(bsz * seq, ATT_WIDTH)


def _attention_bias(rel_bias):
    heads = rel_bias.shape[0]
    rb = rel_bias.astype(F32) * LOG2_E
    d_min = N_PREV_CHUNKS * CHUNK - (K_WINDOW - 1)
    d_max = N_PREV_CHUNKS * CHUNK + Q_TILE - 1
    n = d_max - d_min + 1
    by_dist = jnp.concatenate([jnp.repeat(rb[:, :1], -REL_CLIP - d_min, axis=1), rb,
                               jnp.repeat(rb[:, -1:], d_max - REL_CLIP, axis=1)], axis=1)
    rev = jnp.concatenate([by_dist[:, ::-1], jnp.zeros((heads, 1), F32)], axis=1)
    skew = jnp.tile(rev, (1, Q_TILE))[:, :Q_TILE * n].reshape(heads, Q_TILE, n)
    table = skew[:, :, Q_TILE - 1:Q_TILE - 1 + K_WINDOW]
    qc = jnp.arange(Q_TILE)[:, None] // CHUNK
    kc = jnp.arange(K_WINDOW)[None, :] // CHUNK
    in_band = (kc >= qc) & (kc <= qc + N_PREV_CHUNKS)
    return jnp.where(in_band[None], table, NEG_INF)


def _cmul(ar, ai, br, bi):
    return ar * br - ai * bi, ar * bi + ai * br


def _ssm_kernel(n_chunk, u_ref, mloc_ref, we_ref, wc_ref, mu_ref, o_ref):
    rows = u_ref.shape[1] // SSM_T
    half = SSM_GROUPS * SSM_STATE
    pieces = [(s, q) for s in range(SSM_T) for q in range(SSM_SLABS)]
    xb = jnp.concatenate([u_ref[q, pl.ds(s, rows, stride=SSM_T), :] for s, q in pieces], axis=-1).astype(BF16)
    e = _dot(xb, we_ref[...])
    er, ei = e[:, :half], e[:, half:]
    chunk_id = lax.broadcasted_iota(jnp.int32, (rows, half), 0) % n_chunk
    step = 1
    k = 0
    while step < n_chunk:
        pr = jnp.where(chunk_id >= step, pltpu.roll(er, step, 0), 0.0)
        pi = jnp.where(chunk_id >= step, pltpu.roll(ei, step, 0), 0.0)
        pr, pi = _cmul(pr, pi, mu_ref[2 * k:2 * k + 1, :], mu_ref[2 * k + 1:2 * k + 2, :])
        er, ei = er + pr, ei + pi
        step *= 2
        k += 1
    sr = jnp.where(chunk_id >= 1, pltpu.roll(er, 1, 0), 0.0)
    si = jnp.where(chunk_id >= 1, pltpu.roll(ei, 1, 0), 0.0)
    state = jnp.concatenate([sr, si], axis=-1).astype(BF16)
    y = _dot(xb, mloc_ref[...]) + _dot(state, wc_ref[...])
    for j, (s, q) in enumerate(pieces):
        o_ref[q, pl.ds(s, rows, stride=SSM_T), :] = y[:, j * V7X_LANES:(j + 1) * V7X_LANES]


def _ssm(us, mats, bsz, seq):
    n_seq = min(SSM_SEQS, bsz)
    blk = pl.BlockSpec((SSM_SLABS, n_seq * seq, V7X_LANES), lambda i: (0, i, 0))
    return pl.pallas_call(
        functools.partial(_ssm_kernel, seq // SSM_T),
        grid=(bsz // n_seq,),
        in_specs=[blk] + [_resident(m.shape) for m in mats],
        out_specs=blk,
        out_shape=jax.ShapeDtypeStruct(us.shape, F32),
        compiler_params=_params(1),
        name="s5_scan",
    )(us, *mats)


def _ssm_matrices(a_re, a_im, log_dt, b_re, b_im, c_re, c_im, n_chunk):
    hi = lax.Precision.HIGHEST
    dt = jnp.exp(log_dt.astype(F32))[:, None]
    a_re = a_re.astype(F32)
    a_im = a_im.astype(F32)
    mag = jnp.exp(dt * a_re)
    ang = dt * a_im
    ab_re = mag * jnp.cos(ang)
    ab_im = mag * jnp.sin(ang)
    den = a_re * a_re + a_im * a_im
    nr = ab_re - 1.0
    ni = ab_im
    coef_re = (nr * a_re + ni * a_im) / den
    coef_im = (ni * a_re - nr * a_im) / den
    b_re = b_re.astype(F32)
    b_im = b_im.astype(F32)
    bb_re = coef_re[..., None] * b_re - coef_im[..., None] * b_im
    bb_im = coef_re[..., None] * b_im + coef_im[..., None] * b_re
    c_re = c_re.astype(F32)
    c_im = c_im.astype(F32)
    pw_r, pw_i = [jnp.ones_like(ab_re)], [jnp.zeros_like(ab_im)]
    for _ in range(SSM_T):
        r, i = _cmul(pw_r[-1], pw_i[-1], ab_re, ab_im)
        pw_r.append(r)
        pw_i.append(i)
    pw_r, pw_i = jnp.stack(pw_r), jnp.stack(pw_i)
    half = SSM_GROUPS * SSM_STATE
    width = SSM_T * SSM_WIDTH

    def spread(m):
        tiled = jnp.tile(m, (1, 1, SSM_GROUPS))
        row_g = jnp.arange(m.shape[1]) // (m.shape[1] // SSM_GROUPS)
        col_g = jnp.arange(tiled.shape[2]) // m.shape[2]
        return jnp.where(row_g[:, None] == col_g[None, :], tiled, 0.0)

    cp_r = c_re[None] * pw_r[:, :, None, :] - c_im[None] * pw_i[:, :, None, :]
    cp_i = c_re[None] * pw_i[:, :, None, :] + c_im[None] * pw_r[:, :, None, :]
    kern = (jnp.einsum('tghp,gpk->tgkh', cp_r[:SSM_T], bb_re, precision=hi)
            - jnp.einsum('tghp,gpk->tgkh', cp_i[:SSM_T], bb_im, precision=hi))
    lag_blocks = spread(kern.reshape(SSM_T, SSM_WIDTH, SSM_GROUP))
    no_block = jnp.zeros((SSM_WIDTH, SSM_WIDTH), F32)
    mloc = jnp.block([[lag_blocks[t - s] if t >= s else no_block for t in range(SSM_T)]
                      for s in range(SSM_T)])

    rev_r, rev_i = pw_r[SSM_T - 1::-1], pw_i[SSM_T - 1::-1]
    pb_r = rev_r[..., None] * bb_re[None] - rev_i[..., None] * bb_im[None]
    pb_i = rev_r[..., None] * bb_im[None] + rev_i[..., None] * bb_re[None]

    def drive(m):
        m = jnp.transpose(m, (0, 1, 3, 2)).reshape(SSM_T, SSM_WIDTH, SSM_STATE)
        return spread(m).reshape(width, half)

    we = jnp.concatenate([drive(pb_r), drive(pb_i)], axis=1)

    def read(m):
        m = jnp.transpose(m[1:], (0, 1, 3, 2)).reshape(SSM_T, half, SSM_GROUP)
        return jnp.concatenate(list(spread(m)), axis=1)

    wc = jnp.concatenate([read(cp_r), -read(cp_i)], axis=0)

    pr, pi = pw_r[SSM_T].reshape(1, half), pw_i[SSM_T].reshape(1, half)
    mus = []
    step = 1
    while step < n_chunk:
        mus += [pr, pi]
        pr, pi = _cmul(pr, pi, pr, pi)
        step *= 2
    return mloc.astype(BF16), we.astype(BF16), wc.astype(BF16), jnp.concatenate(mus, axis=0)


def _ab_out_kernel(x_ref, ya_ref, ys_ref, us_ref, d_ref, wglu_ref, bglu_ref, wa_ref, wb_ref, g_ref, b_ref, o_ref):
    ys = jnp.concatenate([ys_ref[q] for q in range(SSM_SLABS)], axis=-1)
    us = jnp.concatenate([us_ref[q] for q in range(SSM_SLABS)], axis=-1)
    y = ys + d_ref[...] * us
    gl = _gelu_tanh(y)
    yb = gl * jax.nn.sigmoid(_dot(gl.astype(BF16), wglu_ref[...]) + bglu_ref[...])
    mix = _dot(ya_ref[...], wa_ref[...]) + _dot(yb.astype(BF16), wb_ref[...])
    o_ref[...] = _layer_norm(ALPHA * x_ref[...] + mix, g_ref[...], b_ref[...])


def _ab_out(x, ya, ys, us, d, wglu, bglu, wa, wb, g, b):
    n = x.shape[0]

    def rows(width):
        return pl.BlockSpec((ROW_TILE, width), lambda i: (i, 0))

    consts = (d, wglu, bglu, wa, wb, g, b)
    slabs = pl.BlockSpec((SSM_SLABS, ROW_TILE, V7X_LANES), lambda i: (0, i, 0))
    return pl.pallas_call(
        _ab_out_kernel,
        grid=(n // ROW_TILE,),
        in_specs=[rows(D_MODEL), rows(ATT_WIDTH), slabs, slabs] + [_resident(c.shape) for c in consts],
        out_specs=rows(D_MODEL),
        out_shape=jax.ShapeDtypeStruct((n, D_MODEL), F32),
        compiler_params=_params(1),
        name="ab_out_ln",
    )(x, ya, ys, us, *consts)


def _cd_kernel(seq, x_ref, halo_ref, win_ref, pw_ref, ps_ref, sg_ref, sb_ref, ws_ref, bs_ref,
               wout_ref, g_ref, b_ref, o_ref):
    x = x_ref[...]
    h = _dot(x.astype(BF16), win_ref[...])
    xc = h[:, :POOL_WIDTH]
    row0 = (pl.program_id(0) * ROW_TILE) % seq
    hc = _dot(halo_ref[...].astype(BF16), win_ref[:, :POOL_WIDTH])
    hc = jnp.where(row0 == 0, 0.0, hc)
    ext = jnp.concatenate([hc, xc], axis=0)
    pos = row0 + lax.broadcasted_iota(jnp.int32, (ROW_TILE, POOL_GROUP), 0)
    acc = ext
    ys = []
    for gi, w in enumerate(POOL_WINDOWS):
        acc = acc + pltpu.roll(acc, w // 2, 0)
        cnt = jnp.minimum(pos + 1, w).astype(F32)
        pooled = acc[POOL_HALO:, :POOL_GROUP] / cnt - xc[:, gi * POOL_GROUP:(gi + 1) * POOL_GROUP]
        ys.append(_dot(pooled.astype(BF16), pw_ref[gi]))
        if gi + 1 < len(POOL_WINDOWS):
            acc = acc[:, POOL_GROUP:]
    yc = jnp.concatenate(ys, axis=-1) * ps_ref[...]

    z = _gelu_tanh(h[:, POOL_WIDTH:])
    u = z[:, :SGU_WIDTH]
    v = _layer_norm(z[:, SGU_WIDTH:], sg_ref[...], sb_ref[...]).astype(BF16)
    hd = SGU_WIDTH // SGU_HEADS
    blocks = []
    for r in range(ROW_TILE // SGU_CHUNK):
        rs = slice(r * SGU_CHUNK, (r + 1) * SGU_CHUNK)
        heads = [_dot(ws_ref[hh], v[rs, hh * hd:(hh + 1) * hd]) for hh in range(SGU_HEADS)]
        blocks.append(jnp.concatenate(heads, axis=-1) + bs_ref[...])
    yd = u * jnp.concatenate(blocks, axis=0)

    mix = _dot(yc.astype(BF16), wout_ref[:POOL_WIDTH, :]) + _dot(yd.astype(BF16), wout_ref[POOL_WIDTH:, :])
    o_ref[...] = _layer_norm(ALPHA * x + mix, g_ref[...], b_ref[...])


def _cd_mixer(x, seq, win, pw, ps, sg, sb, ws, bs, wout, g, b):
    n = x.shape[0]
    halo_per_tile = ROW_TILE // POOL_HALO
    row = pl.BlockSpec((ROW_TILE, D_MODEL), lambda i: (i, 0))
    halo = pl.BlockSpec((POOL_HALO, D_MODEL), lambda i: (jnp.maximum(i * halo_per_tile - 1, 0), 0))
    consts = (win, pw, ps, sg, sb, ws, bs, wout, g, b)
    return pl.pallas_call(
        functools.partial(_cd_kernel, seq),
        grid=(n // ROW_TILE,),
        in_specs=[row, halo] + [_resident(c.shape) for c in consts],
        out_specs=row,
        out_shape=jax.ShapeDtypeStruct((n, D_MODEL), F32),
        compiler_params=_params(1),
        name="cd_mixer_ln",
    )(x, x, *consts)


def kernel(x, ln_g, ln_b, ffn_w_gate, ffn_w_up, ffn_w_down, ab_w_in, ab_w_out, att_rel_bias, ssm_a_re, ssm_a_im, ssm_log_dt, ssm_b_re, ssm_b_im, ssm_c_re, ssm_c_im, ssm_d, ssm_w_glu, ssm_b_glu, cd_w_in, cd_w_out, pool_w, pool_scale, sgu_ln_g, sgu_ln_b, sgu_w_s, sgu_b_s):
    bsz, seq, d_model = x.shape
    assert d_model == D_MODEL and seq % ROW_TILE == 0 and seq % Q_TILE == 0 and seq % SSM_T == 0
    assert bsz % min(SSM_SEQS, bsz) == 0 and ln_g.shape[0] == DEPTH
    n = bsz * seq
    h = x.reshape(n, D_MODEL).astype(F32)

    def ln_params(l, j):
        return ln_g[l, j].astype(F32).reshape(1, D_MODEL), ln_b[l, j].astype(F32).reshape(1, D_MODEL)

    wg_all, wu_all, wd_all = ffn_w_gate.astype(BF16), ffn_w_up.astype(BF16), ffn_w_down.astype(BF16)

    def ffn(h, l, j):
        return _ffn_ln(h, wg_all, wu_all, wd_all, l, j, *ln_params(l, j if j == 0 else 2))

    q_scale = jnp.concatenate([jnp.full((ATT_WIDTH,), ATT_HEAD_DIM ** -0.5 * LOG2_E, F32),
                               jnp.ones((2 * ATT_WIDTH + SSM_WIDTH,), F32)])
    tril = jnp.tril(jnp.ones((SGU_CHUNK, SGU_CHUNK), F32))

    for l in range(DEPTH):
        h = ffn(h, l, 0)
        i = l // 2
        if l % 2 == 0:
            qkv, us = _ab_in(h, (ab_w_in[i].astype(F32) * q_scale).astype(BF16))
            ya = _attention(qkv, _attention_bias(att_rel_bias[i]), bsz, seq)
            mats = _ssm_matrices(ssm_a_re[i], ssm_a_im[i], ssm_log_dt[i], ssm_b_re[i], ssm_b_im[i],
                                 ssm_c_re[i], ssm_c_im[i], seq // SSM_T)
            ys = _ssm(us, mats, bsz, seq)
            w_out = ab_w_out[i].astype(BF16)
            h = _ab_out(h, ya, ys, us, ssm_d[i].astype(F32).reshape(1, SSM_WIDTH), ssm_w_glu[i].astype(BF16),
                        ssm_b_glu[i].astype(F32).reshape(1, SSM_WIDTH), w_out[:ATT_WIDTH], w_out[ATT_WIDTH:],
                        *ln_params(l, 1))
        else:
            bs_tbl = jnp.repeat(sgu_b_s[i].astype(F32).T, SGU_WIDTH // SGU_HEADS, axis=1)
            h = _cd_mixer(h, seq, cd_w_in[i].astype(BF16), pool_w[i].astype(BF16),
                          pool_scale[i].astype(F32).reshape(1, POOL_WIDTH),
                          sgu_ln_g[i].astype(F32).reshape(1, SGU_WIDTH), sgu_ln_b[i].astype(F32).reshape(1, SGU_WIDTH),
                          (sgu_w_s[i].astype(F32) * tril).astype(BF16), bs_tbl,
                          cd_w_out[i].astype(BF16), *ln_params(l, 1))
        h = ffn(h, l, 1)
    return h.reshape(bsz, seq, D_MODEL).astype(x.dtype)
```

```python
import functools

import jax
import jax.numpy as jnp
from jax import lax
from jax.experimental import pallas as pl
from jax.experimental.pallas import tpu as pltpu

BF16 = jnp.bfloat16
F32 = jnp.float32

DEPTH = 4
D_MODEL = 1024
D_FF = 2816
ALPHA = (2 * DEPTH) ** 0.25
LN_EPS = 1e-5
NEG_INF = -1e30
LOG2_E = 1.4426950408889634

CHUNK = 64
N_PREV_CHUNKS = 8
REL_CLIP = 128
ATT_HEAD_DIM = 64
ATT_WIDTH = 768
ATT_HEADS = 12
SSM_WIDTH = 256
SSM_GROUP = 16
SSM_GROUPS = 16
SSM_STATE = 64
POOL_WINDOWS = (2, 4, 8, 16)
POOL_WIDTH = 512
POOL_GROUP = 128
SGU_WIDTH = 512
SGU_CHUNK = 128
SGU_HEADS = 4

V7X_LANES = 128
V7X_MXU_DIM = 256
V7X_VMEM_LIMIT_BYTES = 60000 * 1024

ROW_TILE = 512
FF_CHUNK = V7X_MXU_DIM
Q_TILE = 4 * CHUNK
K_WINDOW = Q_TILE + N_PREV_CHUNKS * CHUNK
SSM_T = 8
SSM_SEQS = 2
SSM_SLABS = SSM_WIDTH // V7X_LANES
POOL_HALO = 16


def _params(n_axes):
    return pltpu.CompilerParams(
        dimension_semantics=("arbitrary",) * n_axes,
        vmem_limit_bytes=V7X_VMEM_LIMIT_BYTES)


def _resident(shape):
    nd = len(shape)
    return pl.BlockSpec(shape, lambda *_: (0,) * nd, pipeline_mode=pl.Buffered(1))


def _layer_norm(z, g, b):
    mu = jnp.mean(z, axis=-1, keepdims=True)
    zc = z - mu
    var = jnp.mean(zc * zc, axis=-1, keepdims=True)
    return zc * lax.rsqrt(var + LN_EPS) * g + b


def _gelu_tanh(x):
    return 0.5 * x * (1.0 + jnp.tanh(0.7978845608028654 * (x + 0.044715 * (x * x * x))))


def _dot(a, b):
    return jnp.dot(a, b, preferred_element_type=F32)


def _ffn_kernel(n_tiles, n_pre, pre_fn, *refs):
    pre_refs = refs[:n_pre]
    wg_ref, wu_ref, wd_ref, g_ref, b_ref, o_ref, h_scr, x_scr, y_scr = refs[n_pre:]
    i = pl.program_id(0)

    def branch_into_scratch():
        x = pre_fn(*pre_refs)
        xb = x.astype(BF16)
        for c in range(D_FF // FF_CHUNK):
            sl = slice(c * FF_CHUNK, (c + 1) * FF_CHUNK)
            gate = _dot(xb, wg_ref[:, sl])
            up = _dot(xb, wu_ref[:, sl])
            h_scr[:, sl] = (gate * jax.nn.sigmoid(gate) * up).astype(BF16)
        y = _dot(h_scr[...], wd_ref[...])
        x_scr[...] = x
        y_scr[...] = y

    def finish_previous():
        o_ref[...] = _layer_norm(ALPHA * x_scr[...] + 0.5 * y_scr[...], g_ref[...], b_ref[...])

    @pl.when(i == 0)
    def _():
        branch_into_scratch()

    @pl.when(jnp.logical_and(i > 0, i < n_tiles))
    def _():
        finish_previous()
        branch_into_scratch()

    @pl.when(i == n_tiles)
    def _():
        finish_previous()


def _tile_rows(width, n_tiles):
    return pl.BlockSpec((ROW_TILE, width), lambda i: (jnp.minimum(i, n_tiles - 1), 0))


def _tile_slabs(n_tiles):
    return pl.BlockSpec((SSM_SLABS, ROW_TILE, V7X_LANES), lambda i: (0, jnp.minimum(i, n_tiles - 1), 0))


def _load_tile(x_ref):
    return x_ref[...]


def _ffn_ln(n, pre_fn, pre_args, pre_specs, wg, wu, wd, layer, pos, g, b, name):
    n_tiles = n // ROW_TILE

    def picked(rows, cols):
        return pl.BlockSpec((None, None, rows, cols), lambda i: (layer, pos, 0, 0), pipeline_mode=pl.Buffered(1))

    return pl.pallas_call(
        functools.partial(_ffn_kernel, n_tiles, len(pre_args), pre_fn),
        grid=(n_tiles + 1,),
        in_specs=list(pre_specs) + [picked(D_MODEL, D_FF), picked(D_MODEL, D_FF), picked(D_FF, D_MODEL),
                                    _resident((1, D_MODEL)), _resident((1, D_MODEL))],
        out_specs=pl.BlockSpec((ROW_TILE, D_MODEL), lambda i: (jnp.maximum(i - 1, 0), 0)),
        out_shape=jax.ShapeDtypeStruct((n, D_MODEL), F32),
        scratch_shapes=[pltpu.VMEM((ROW_TILE, D_FF), BF16), pltpu.VMEM((ROW_TILE, D_MODEL), F32),
                        pltpu.VMEM((ROW_TILE, D_MODEL), F32)],
        compiler_params=_params(1),
        name=name,
    )(*pre_args, wg, wu, wd, g, b)


def _ab_in_kernel(x_ref, w_ref, qkv_ref, us_ref):
    h = _dot(x_ref[...].astype(BF16), w_ref[...])
    qkv_ref[...] = h[:, :3 * ATT_WIDTH].astype(BF16)
    for q in range(SSM_SLABS):
        lo = 3 * ATT_WIDTH + q * V7X_LANES
        us_ref[q] = h[:, lo:lo + V7X_LANES]


def _ab_in(x, w):
    n = x.shape[0]
    width = 3 * ATT_WIDTH + SSM_WIDTH
    return pl.pallas_call(
        _ab_in_kernel,
        grid=(n // ROW_TILE,),
        in_specs=[pl.BlockSpec((ROW_TILE, D_MODEL), lambda i: (i, 0)), _resident((D_MODEL, width))],
        out_specs=[pl.BlockSpec((ROW_TILE, 3 * ATT_WIDTH), lambda i: (i, 0)),
                   pl.BlockSpec((SSM_SLABS, ROW_TILE, V7X_LANES), lambda i: (0, i, 0))],
        out_shape=[jax.ShapeDtypeStruct((n, 3 * ATT_WIDTH), BF16),
                   jax.ShapeDtypeStruct((SSM_SLABS, n, V7X_LANES), F32)],
        compiler_params=_params(1),
        name="ab_in",
    )(x, w)


def _attn_kernel(q_ref, k_ref, v_ref, bias_ref, o_ref):
    seq = q_ref.shape[0]
    lane = lax.broadcasted_iota(jnp.int32, (Q_TILE, V7X_LANES), 1)
    first_head = lane < ATT_HEAD_DIM
    for t in range(seq // Q_TILE):
        q0 = t * Q_TILE
        k0 = max(0, q0 + Q_TILE - K_WINDOW)
        width = q0 + Q_TILE - k0
        q = q_ref[q0:q0 + Q_TILE, :]
        k = k_ref[k0:k0 + width, :]
        v = v_ref[k0:k0 + width, :]
        zero = jnp.zeros_like(q)
        qs = jnp.concatenate([jnp.where(first_head, q, zero), jnp.where(first_head, zero, q)], axis=0)
        s = lax.dot_general(qs, k, (((1,), (1,)), ((), ())), preferred_element_type=F32)
        s = s + bias_ref[:, :, K_WINDOW - width:].reshape(2 * Q_TILE, width)
        p = jnp.exp2(s - jnp.max(s, axis=-1, keepdims=True))
        o = _dot(p.astype(BF16), v) / jnp.sum(p, axis=-1, keepdims=True)
        o_ref[q0:q0 + Q_TILE, :] = jnp.where(first_head, o[:Q_TILE], o[Q_TILE:]).astype(BF16)


def _attention(qkv, bias, bsz, seq):
    pairs = ATT_WIDTH // V7X_LANES
    qkv3 = qkv.reshape(bsz, seq, 3 * ATT_WIDTH)

    def col_block(offset):
        return pl.BlockSpec((None, seq, V7X_LANES), lambda b, j: (b, 0, offset + j))

    out = pl.pallas_call(
        _attn_kernel,
        grid=(bsz, pairs),
        in_specs=[col_block(0), col_block(pairs), col_block(2 * pairs),
                  pl.BlockSpec((2, Q_TILE, K_WINDOW), lambda b, j: (j, 0, 0))],
        out_specs=pl.BlockSpec((None, seq, V7X_LANES), lambda b, j: (b, 0, j)),
        out_shape=jax.ShapeDtypeStruct((bsz, seq, ATT_WIDTH), BF16),
        compiler_params=_params(2),
        name="band_attention",
    )(qkv3, qkv3, qkv3, bias)
    return out.reshape(bsz * seq, ATT_WIDTH)


def _attention_bias(rel_bias):
    heads = rel_bias.shape[0]
    rb = rel_bias.astype(F32) * LOG2_E
    d_min = N_PREV_CHUNKS * CHUNK - (K_WINDOW - 1)
    d_max = N_PREV_CHUNKS * CHUNK + Q_TILE - 1
    n = d_max - d_min + 1
    by_dist = jnp.concatenate([jnp.repeat(rb[:, :1], -REL_CLIP - d_min, axis=1), rb,
                               jnp.repeat(rb[:, -1:], d_max - REL_CLIP, axis=1)], axis=1)
    rev = jnp.concatenate([by_dist[:, ::-1], jnp.zeros((heads, 1), F32)], axis=1)
    skew = jnp.tile(rev, (1, Q_TILE))[:, :Q_TILE * n].reshape(heads, Q_TILE, n)
    table = skew[:, :, Q_TILE - 1:Q_TILE - 1 + K_WINDOW]
    qc = jnp.arange(Q_TILE)[:, None] // CHUNK
    kc = jnp.arange(K_WINDOW)[None, :] // CHUNK
    in_band = (kc >= qc) & (kc <= qc + N_PREV_CHUNKS)
    return jnp.where(in_band[None], table, NEG_INF)


def _cmul(ar, ai, br, bi):
    return ar * br - ai * bi, ar * bi + ai * br


def _ssm_kernel(n_chunk, u_ref, mloc_ref, we_ref, wc_ref, mu_ref, o_ref):
    rows = u_ref.shape[1] // SSM_T
    half = SSM_GROUPS * SSM_STATE
    pieces = [(s, q) for s in range(SSM_T) for q in range(SSM_SLABS)]
    xb = jnp.concatenate([u_ref[q, pl.ds(s, rows, stride=SSM_T), :] for s, q in pieces], axis=-1).astype(BF16)
    e = _dot(xb, we_ref[...])
    er, ei = e[:, :half], e[:, half:]
    chunk_id = lax.broadcasted_iota(jnp.int32, (rows, half), 0) % n_chunk
    step = 1
    k = 0
    while step < n_chunk:
        pr = jnp.where(chunk_id >= step, pltpu.roll(er, step, 0), 0.0)
        pi = jnp.where(chunk_id >= step, pltpu.roll(ei, step, 0), 0.0)
        pr, pi = _cmul(pr, pi, mu_ref[2 * k:2 * k + 1, :], mu_ref[2 * k + 1:2 * k + 2, :])
        er, ei = er + pr, ei + pi
        step *= 2
        k += 1
    sr = jnp.where(chunk_id >= 1, pltpu.roll(er, 1, 0), 0.0)
    si = jnp.where(chunk_id >= 1, pltpu.roll(ei, 1, 0), 0.0)
    state = jnp.concatenate([sr, si], axis=-1).astype(BF16)
    y = _dot(xb, mloc_ref[...]) + _dot(state, wc_ref[...])
    for j, (s, q) in enumerate(pieces):
        o_ref[q, pl.ds(s, rows, stride=SSM_T), :] = y[:, j * V7X_LANES:(j + 1) * V7X_LANES]


def _ssm(us, mats, bsz, seq):
    n_seq = min(SSM_SEQS, bsz)
    blk = pl.BlockSpec((SSM_SLABS, n_seq * seq, V7X_LANES), lambda i: (0, i, 0))
    return pl.pallas_call(
        functools.partial(_ssm_kernel, seq // SSM_T),
        grid=(bsz // n_seq,),
        in_specs=[blk] + [_resident(m.shape) for m in mats],
        out_specs=blk,
        out_shape=jax.ShapeDtypeStruct(us.shape, F32),
        compiler_params=_params(1),
        name="s5_scan",
    )(us, *mats)


def _ssm_matrices(a_re, a_im, log_dt, b_re, b_im, c_re, c_im, n_chunk):
    hi = lax.Precision.HIGHEST
    dt = jnp.exp(log_dt.astype(F32))[:, None]
    a_re = a_re.astype(F32)
    a_im = a_im.astype(F32)
    mag = jnp.exp(dt * a_re)
    ang = dt * a_im
    ab_re = mag * jnp.cos(ang)
    ab_im = mag * jnp.sin(ang)
    den = a_re * a_re + a_im * a_im
    nr = ab_re - 1.0
    ni = ab_im
    coef_re = (nr * a_re + ni * a_im) / den
    coef_im = (ni * a_re - nr * a_im) / den
    b_re = b_re.astype(F32)
    b_im = b_im.astype(F32)
    bb_re = coef_re[..., None] * b_re - coef_im[..., None] * b_im
    bb_im = coef_re[..., None] * b_im + coef_im[..., None] * b_re
    c_re = c_re.astype(F32)
    c_im = c_im.astype(F32)
    pw_r, pw_i = [jnp.ones_like(ab_re)], [jnp.zeros_like(ab_im)]
    for _ in range(SSM_T):
        r, i = _cmul(pw_r[-1], pw_i[-1], ab_re, ab_im)
        pw_r.append(r)
        pw_i.append(i)
    pw_r, pw_i = jnp.stack(pw_r), jnp.stack(pw_i)
    half = SSM_GROUPS * SSM_STATE
    width = SSM_T * SSM_WIDTH

    def spread(m):
        tiled = jnp.tile(m, (1, 1, SSM_GROUPS))
        row_g = jnp.arange(m.shape[1]) // (m.shape[1] // SSM_GROUPS)
        col_g = jnp.arange(tiled.shape[2]) // m.shape[2]
        return jnp.where(row_g[:, None] == col_g[None, :], tiled, 0.0)

    cp_r = c_re[None] * pw_r[:, :, None, :] - c_im[None] * pw_i[:, :, None, :]
    cp_i = c_re[None] * pw_i[:, :, None, :] + c_im[None] * pw_r[:, :, None, :]
    kern = (jnp.einsum('tghp,gpk->tgkh', cp_r[:SSM_T], bb_re, precision=hi)
            - jnp.einsum('tghp,gpk->tgkh', cp_i[:SSM_T], bb_im, precision=hi))
    lag_blocks = spread(kern.reshape(SSM_T, SSM_WIDTH, SSM_GROUP))
    no_block = jnp.zeros((SSM_WIDTH, SSM_WIDTH), F32)
    mloc = jnp.block([[lag_blocks[t - s] if t >= s else no_block for t in range(SSM_T)]
                      for s in range(SSM_T)])

    rev_r, rev_i = pw_r[SSM_T - 1::-1], pw_i[SSM_T - 1::-1]
    pb_r = rev_r[..., None] * bb_re[None] - rev_i[..., None] * bb_im[None]
    pb_i = rev_r[..., None] * bb_im[None] + rev_i[..., None] * bb_re[None]

    def drive(m):
        m = jnp.transpose(m, (0, 1, 3, 2)).reshape(SSM_T, SSM_WIDTH, SSM_STATE)
        return spread(m).reshape(width, half)

    we = jnp.concatenate([drive(pb_r), drive(pb_i)], axis=1)

    def read(m):
        m = jnp.transpose(m[1:], (0, 1, 3, 2)).reshape(SSM_T, half, SSM_GROUP)
        return jnp.concatenate(list(spread(m)), axis=1)

    wc = jnp.concatenate([read(cp_r), -read(cp_i)], axis=0)

    pr, pi = pw_r[SSM_T].reshape(1, half), pw_i[SSM_T].reshape(1, half)
    mus = []
    step = 1
    while step < n_chunk:
        mus += [pr, pi]
        pr, pi = _cmul(pr, pi, pr, pi)
        step *= 2
    return mloc.astype(BF16), we.astype(BF16), wc.astype(BF16), jnp.concatenate(mus, axis=0)


def _ab_out_tile(x_ref, ya_ref, ys_ref, us_ref, d_ref, wglu_ref, bglu_ref, wa_ref, wb_ref, g_ref, b_ref):
    ys = jnp.concatenate([ys_ref[q] for q in range(SSM_SLABS)], axis=-1)
    us = jnp.concatenate([us_ref[q] for q in range(SSM_SLABS)], axis=-1)
    y = ys + d_ref[...] * us
    gl = _gelu_tanh(y)
    yb = gl * jax.nn.sigmoid(_dot(gl.astype(BF16), wglu_ref[...]) + bglu_ref[...])
    mix = _dot(ya_ref[...], wa_ref[...]) + _dot(yb.astype(BF16), wb_ref[...])
    return _layer_norm(ALPHA * x_ref[...] + mix, g_ref[...], b_ref[...])


def _ab_out_operands(x, ya, ys, us, d, wglu, bglu, wa, wb, g, b):
    n_tiles = x.shape[0] // ROW_TILE
    consts = (d, wglu, bglu, wa, wb, g, b)
    specs = [_tile_rows(D_MODEL, n_tiles), _tile_rows(ATT_WIDTH, n_tiles), _tile_slabs(n_tiles),
             _tile_slabs(n_tiles)] + [_resident(c.shape) for c in consts]
    return (x, ya, ys, us) + consts, specs


def _cd_tile(seq, x_ref, halo_ref, win_ref, pw_ref, ps_ref, sg_ref, sb_ref, ws_ref, bs_ref,
             wout_ref, g_ref, b_ref):
    x = x_ref[...]
    h = _dot(x.astype(BF16), win_ref[...])
    xc = h[:, :POOL_WIDTH]
    row0 = (pl.program_id(0) * ROW_TILE) % seq
    hc = _dot(halo_ref[...].astype(BF16), win_ref[:, :POOL_WIDTH])
    hc = jnp.where(row0 == 0, 0.0, hc)
    ext = jnp.concatenate([hc, xc], axis=0)
    pos = row0 + lax.broadcasted_iota(jnp.int32, (ROW_TILE, POOL_GROUP), 0)
    acc = ext
    ys = []
    for gi, w in enumerate(POOL_WINDOWS):
        acc = acc + pltpu.roll(acc, w // 2, 0)
        cnt = jnp.minimum(pos + 1, w).astype(F32)
        pooled = acc[POOL_HALO:, :POOL_GROUP] / cnt - xc[:, gi * POOL_GROUP:(gi + 1) * POOL_GROUP]
        ys.append(_dot(pooled.astype(BF16), pw_ref[gi]))
        if gi + 1 < len(POOL_WINDOWS):
            acc = acc[:, POOL_GROUP:]
    yc = jnp.concatenate(ys, axis=-1) * ps_ref[...]

    z = _gelu_tanh(h[:, POOL_WIDTH:])
    u = z[:, :SGU_WIDTH]
    v = _layer_norm(z[:, SGU_WIDTH:], sg_ref[...], sb_ref[...]).astype(BF16)
    hd = SGU_WIDTH // SGU_HEADS
    blocks = []
    for r in range(ROW_TILE // SGU_CHUNK):
        rs = slice(r * SGU_CHUNK, (r + 1) * SGU_CHUNK)
        heads = [_dot(ws_ref[hh], v[rs, hh * hd:(hh + 1) * hd]) for hh in range(SGU_HEADS)]
        blocks.append(jnp.concatenate(heads, axis=-1) + bs_ref[...])
    yd = u * jnp.concatenate(blocks, axis=0)

    mix = _dot(yc.astype(BF16), wout_ref[:POOL_WIDTH, :]) + _dot(yd.astype(BF16), wout_ref[POOL_WIDTH:, :])
    return _layer_norm(ALPHA * x + mix, g_ref[...], b_ref[...])


def _cd_operands(x, win, pw, ps, sg, sb, ws, bs, wout, g, b):
    n_tiles = x.shape[0] // ROW_TILE
    halo_per_tile = ROW_TILE // POOL_HALO
    halo = pl.BlockSpec((POOL_HALO, D_MODEL),
                        lambda i: (jnp.maximum(jnp.minimum(i, n_tiles - 1) * halo_per_tile - 1, 0), 0))
    consts = (win, pw, ps, sg, sb, ws, bs, wout, g, b)
    return (x, x) + consts, [_tile_rows(D_MODEL, n_tiles), halo] + [_resident(c.shape) for c in consts]


def kernel(x, ln_g, ln_b, ffn_w_gate, ffn_w_up, ffn_w_down, ab_w_in, ab_w_out, att_rel_bias, ssm_a_re, ssm_a_im, ssm_log_dt, ssm_b_re, ssm_b_im, ssm_c_re, ssm_c_im, ssm_d, ssm_w_glu, ssm_b_glu, cd_w_in, cd_w_out, pool_w, pool_scale, sgu_ln_g, sgu_ln_b, sgu_w_s, sgu_b_s):
    bsz, seq, d_model = x.shape
    assert d_model == D_MODEL and seq % ROW_TILE == 0 and seq % Q_TILE == 0 and seq % SSM_T == 0
    assert bsz % min(SSM_SEQS, bsz) == 0 and ln_g.shape[0] == DEPTH
    n = bsz * seq
    h = x.reshape(n, D_MODEL).astype(F32)

    def ln_params(l, j):
        return ln_g[l, j].astype(F32).reshape(1, D_MODEL), ln_b[l, j].astype(F32).reshape(1, D_MODEL)

    wg_all, wu_all, wd_all = ffn_w_gate.astype(BF16), ffn_w_up.astype(BF16), ffn_w_down.astype(BF16)

    def ffn(pre_fn, operands, l, j, name):
        pre_args, pre_specs = operands
        return _ffn_ln(n, pre_fn, pre_args, pre_specs, wg_all, wu_all, wd_all, l, j,
                       *ln_params(l, j if j == 0 else 2), name)

    q_scale = jnp.concatenate([jnp.full((ATT_WIDTH,), ATT_HEAD_DIM ** -0.5 * LOG2_E, F32),
                               jnp.ones((2 * ATT_WIDTH + SSM_WIDTH,), F32)])
    tril = jnp.tril(jnp.ones((SGU_CHUNK, SGU_CHUNK), F32))

    for l in range(DEPTH):
        h = ffn(_load_tile, ((h,), [_tile_rows(D_MODEL, n // ROW_TILE)]), l, 0, "ffn_ln")
        i = l // 2
        if l % 2 == 0:
            qkv, us = _ab_in(h, (ab_w_in[i].astype(F32) * q_scale).astype(BF16))
            ya = _attention(qkv, _attention_bias(att_rel_bias[i]), bsz, seq)
            mats = _ssm_matrices(ssm_a_re[i], ssm_a_im[i], ssm_log_dt[i], ssm_b_re[i], ssm_b_im[i],
                                 ssm_c_re[i], ssm_c_im[i], seq // SSM_T)
            ys = _ssm(us, mats, bsz, seq)
            w_out = ab_w_out[i].astype(BF16)
            operands = _ab_out_operands(
                h, ya, ys, us, ssm_d[i].astype(F32).reshape(1, SSM_WIDTH), ssm_w_glu[i].astype(BF16),
                ssm_b_glu[i].astype(F32).reshape(1, SSM_WIDTH), w_out[:ATT_WIDTH], w_out[ATT_WIDTH:],
                *ln_params(l, 1))
            h = ffn(_ab_out_tile, operands, l, 1, "ab_out_ffn_ln")
        else:
            bs_tbl = jnp.repeat(sgu_b_s[i].astype(F32).T, SGU_WIDTH // SGU_HEADS, axis=1)
            operands = _cd_operands(
                h, cd_w_in[i].astype(BF16), pool_w[i].astype(BF16), pool_scale[i].astype(F32).reshape(1, POOL_WIDTH),
                sgu_ln_g[i].astype(F32).reshape(1, SGU_WIDTH), sgu_ln_b[i].astype(F32).reshape(1, SGU_WIDTH),
                (sgu_w_s[i].astype(F32) * tril).astype(BF16), bs_tbl, cd_w_out[i].astype(BF16), *ln_params(l, 1))
            h = ffn(functools.partial(_cd_tile, seq), operands, l, 1, "cd_ffn_ln")
    return h.reshape(bsz, seq, D_MODEL).astype(x.dtype)
```

```python
import functools

import jax
import jax.numpy as jnp
from jax import lax
from jax.experimental import pallas as pl
from jax.experimental.pallas import tpu as pltpu

BF16 = jnp.bfloat16
F32 = jnp.float32

DEPTH = 4
D_MODEL = 1024
D_FF = 2816
ALPHA = (2 * DEPTH) ** 0.25
LN_EPS = 1e-5
NEG_INF = -1e30
LOG2_E = 1.4426950408889634

CHUNK = 64
N_PREV_CHUNKS = 8
REL_CLIP = 128
ATT_HEAD_DIM = 64
ATT_WIDTH = 768
ATT_HEADS = 12
SSM_WIDTH = 256
SSM_GROUP = 16
SSM_GROUPS = 16
SSM_STATE = 64
POOL_WINDOWS = (2, 4, 8, 16)
POOL_WIDTH = 512
POOL_GROUP = 128
SGU_WIDTH = 512
SGU_CHUNK = 128
SGU_HEADS = 4

V7X_LANES = 128
V7X_MXU_DIM = 256
V7X_VMEM_LIMIT_BYTES = 60000 * 1024

ROW_TILE = 512
FF_CHUNK = V7X_MXU_DIM
Q_TILE = 4 * CHUNK
K_WINDOW = Q_TILE + N_PREV_CHUNKS * CHUNK
SSM_T = 8
SSM_SEQS = 2
SSM_SLABS = SSM_WIDTH // V7X_LANES
POOL_HALO = 16


def _params(n_axes):
    return pltpu.CompilerParams(
        dimension_semantics=("arbitrary",) * n_axes,
        vmem_limit_bytes=V7X_VMEM_LIMIT_BYTES)


def _resident(shape):
    nd = len(shape)
    return pl.BlockSpec(shape, lambda *_: (0,) * nd, pipeline_mode=pl.Buffered(1))


def _layer_norm(z, g, b):
    mu = jnp.mean(z, axis=-1, keepdims=True)
    zc = z - mu
    var = jnp.mean(zc * zc, axis=-1, keepdims=True)
    return zc * lax.rsqrt(var + LN_EPS) * g + b


def _gelu_tanh(x):
    return 0.5 * x * (1.0 + jnp.tanh(0.7978845608028654 * (x + 0.044715 * (x * x * x))))


def _dot(a, b):
    return jnp.dot(a, b, preferred_element_type=F32)


def _ffn_kernel(n_tiles, n_pre, pre_fn, *refs):
    pre_refs = refs[:n_pre]
    wg_ref, wu_ref, wd_ref, g_ref, b_ref, o_ref, h_scr, x_scr, y_scr = refs[n_pre:]
    i = pl.program_id(0)

    def branch_into_scratch():
        x = pre_fn(*pre_refs)
        xb = x.astype(BF16)
        for c in range(D_FF // FF_CHUNK):
            sl = slice(c * FF_CHUNK, (c + 1) * FF_CHUNK)
            gate = _dot(xb, wg_ref[:, sl])
            up = _dot(xb, wu_ref[:, sl])
            h_scr[:, sl] = (gate * jax.nn.sigmoid(gate) * up).astype(BF16)
        y = _dot(h_scr[...], wd_ref[...])
        x_scr[...] = x
        y_scr[...] = y

    def finish_previous():
        o_ref[...] = _layer_norm(ALPHA * x_scr[...] + 0.5 * y_scr[...], g_ref[...], b_ref[...])

    @pl.when(i == 0)
    def _():
        branch_into_scratch()

    @pl.when(jnp.logical_and(i > 0, i < n_tiles))
    def _():
        finish_previous()
        branch_into_scratch()

    @pl.when(i == n_tiles)
    def _():
        finish_previous()


def _tile_rows(width, n_tiles):
    return pl.BlockSpec((ROW_TILE, width), lambda i: (jnp.minimum(i, n_tiles - 1), 0))


def _tile_slabs(n_tiles):
    return pl.BlockSpec((SSM_SLABS, ROW_TILE, V7X_LANES), lambda i: (0, jnp.minimum(i, n_tiles - 1), 0))


def _load_tile(x_ref):
    return x_ref[...]


def _ffn_ln(n, pre_fn, pre_args, pre_specs, wg, wu, wd, layer, pos, g, b, name):
    n_tiles = n // ROW_TILE

    def picked(rows, cols):
        return pl.BlockSpec((None, None, rows, cols), lambda i: (layer, pos, 0, 0), pipeline_mode=pl.Buffered(1))

    return pl.pallas_call(
        functools.partial(_ffn_kernel, n_tiles, len(pre_args), pre_fn),
        grid=(n_tiles + 1,),
        in_specs=list(pre_specs) + [picked(D_MODEL, D_FF), picked(D_MODEL, D_FF), picked(D_FF, D_MODEL),
                                    _resident((1, D_MODEL)), _resident((1, D_MODEL))],
        out_specs=pl.BlockSpec((ROW_TILE, D_MODEL), lambda i: (jnp.maximum(i - 1, 0), 0)),
        out_shape=jax.ShapeDtypeStruct((n, D_MODEL), F32),
        scratch_shapes=[pltpu.VMEM((ROW_TILE, D_FF), BF16), pltpu.VMEM((ROW_TILE, D_MODEL), F32),
                        pltpu.VMEM((ROW_TILE, D_MODEL), F32)],
        compiler_params=_params(1),
        name=name,
    )(*pre_args, wg, wu, wd, g, b)


def _ab_in_kernel(x_ref, w_ref, qkv_ref, us_ref):
    h = _dot(x_ref[...].astype(BF16), w_ref[...])
    qkv_ref[...] = h[:, :3 * ATT_WIDTH].astype(BF16)
    for q in range(SSM_SLABS):
        lo = 3 * ATT_WIDTH + q * V7X_LANES
        us_ref[q] = h[:, lo:lo + V7X_LANES]


def _ab_in(x, w):
    n = x.shape[0]
    width = 3 * ATT_WIDTH + SSM_WIDTH
    return pl.pallas_call(
        _ab_in_kernel,
        grid=(n // ROW_TILE,),
        in_specs=[pl.BlockSpec((ROW_TILE, D_MODEL), lambda i: (i, 0)), _resident((D_MODEL, width))],
        out_specs=[pl.BlockSpec((ROW_TILE, 3 * ATT_WIDTH), lambda i: (i, 0)),
                   pl.BlockSpec((SSM_SLABS, ROW_TILE, V7X_LANES), lambda i: (0, i, 0))],
        out_shape=[jax.ShapeDtypeStruct((n, 3 * ATT_WIDTH), BF16),
                   jax.ShapeDtypeStruct((SSM_SLABS, n, V7X_LANES), F32)],
        compiler_params=_params(1),
        name="ab_in",
    )(x, w)


def _attn_kernel(q_ref, k_ref, v_ref, bias_ref, o_ref):
    seq = q_ref.shape[0]
    lane = lax.broadcasted_iota(jnp.int32, (Q_TILE, V7X_LANES), 1)
    first_head = lane < ATT_HEAD_DIM
    for t in range(seq // Q_TILE):
        q0 = t * Q_TILE
        k0 = max(0, q0 + Q_TILE - K_WINDOW)
        width = q0 + Q_TILE - k0
        q = q_ref[q0:q0 + Q_TILE, :]
        k = k_ref[k0:k0 + width, :]
        v = v_ref[k0:k0 + width, :]
        zero = jnp.zeros_like(q)
        qs = jnp.concatenate([jnp.where(first_head, q, zero), jnp.where(first_head, zero, q)], axis=0)
        s = lax.dot_general(qs, k, (((1,), (1,)), ((), ())), preferred_element_type=F32)
        s = s + bias_ref[:, :, K_WINDOW - width:].reshape(2 * Q_TILE, width)
        p = jnp.exp2(s - jnp.max(s, axis=-1, keepdims=True))
        o = _dot(p.astype(BF16), v) / jnp.sum(p, axis=-1, keepdims=True)
        o_ref[q0:q0 + Q_TILE, :] = jnp.where(first_head, o[:Q_TILE], o[Q_TILE:]).astype(BF16)


def _attention(qkv, bias, bsz, seq):
    pairs = ATT_WIDTH // V7X_LANES
    qkv3 = qkv.reshape(bsz, seq, 3 * ATT_WIDTH)

    def col_block(offset):
        return pl.BlockSpec((None, seq, V7X_LANES), lambda b, j: (b, 0, offset + j))

    out = pl.pallas_call(
        _attn_kernel,
        grid=(bsz, pairs),
        in_specs=[col_block(0), col_block(pairs), col_block(2 * pairs),
                  pl.BlockSpec((2, Q_TILE, K_WINDOW), lambda b, j: (j, 0, 0))],
        out_specs=pl.BlockSpec((None, seq, V7X_LANES), lambda b, j: (b, 0, j)),
        out_shape=jax.ShapeDtypeStruct((bsz, seq, ATT_WIDTH), BF16),
        compiler_params=_params(2),
        name="band_attention",
    )(qkv3, qkv3, qkv3, bias)
    return out.reshape(bsz * seq, ATT_WIDTH)


def _bias_kernel(rev_ref, o_ref):
    span = rev_ref.shape[-1]
    skew = pltpu.roll(jnp.broadcast_to(rev_ref[0], (Q_TILE, span)), 1, 1, stride=1, stride_axis=0)
    table = skew[:, span - K_WINDOW:]
    qc = lax.broadcasted_iota(jnp.int32, table.shape, 0) // CHUNK
    kc = lax.broadcasted_iota(jnp.int32, table.shape, 1) // CHUNK
    in_band = jnp.logical_and(kc >= qc, kc <= qc + N_PREV_CHUNKS)
    o_ref[0] = jnp.where(in_band, table, NEG_INF)


def _attention_bias(rel_bias):
    heads = rel_bias.shape[0]
    rb = rel_bias.astype(F32) * LOG2_E
    d_min = N_PREV_CHUNKS * CHUNK - (K_WINDOW - 1)
    d_max = N_PREV_CHUNKS * CHUNK + Q_TILE - 1
    span = d_max - d_min + 2
    assert span == Q_TILE + K_WINDOW and span % V7X_LANES == 0
    by_dist = jnp.concatenate([jnp.repeat(rb[:, :1], -REL_CLIP - d_min, axis=1), rb,
                               jnp.repeat(rb[:, -1:], d_max - REL_CLIP, axis=1)], axis=1)
    rev = jnp.concatenate([by_dist[:, ::-1], jnp.zeros((heads, 1), F32)], axis=1).reshape(heads, 1, span)
    return pl.pallas_call(
        _bias_kernel,
        grid=(heads,),
        in_specs=[pl.BlockSpec((1, 1, span), lambda h: (h, 0, 0))],
        out_specs=pl.BlockSpec((1, Q_TILE, K_WINDOW), lambda h: (h, 0, 0)),
        out_shape=jax.ShapeDtypeStruct((heads, Q_TILE, K_WINDOW), F32),
        compiler_params=_params(1),
        name="bias_table",
    )(rev)


def _cmul(ar, ai, br, bi):
    return ar * br - ai * bi, ar * bi + ai * br


def _ssm_kernel(n_chunk, u_ref, mloc_ref, we_ref, wc_ref, mu_ref, o_ref):
    rows = u_ref.shape[1] // SSM_T
    half = SSM_GROUPS * SSM_STATE
    pieces = [(s, q) for s in range(SSM_T) for q in range(SSM_SLABS)]
    xb = jnp.concatenate([u_ref[q, pl.ds(s, rows, stride=SSM_T), :] for s, q in pieces], axis=-1).astype(BF16)
    e = _dot(xb, we_ref[...])
    er, ei = e[:, :half], e[:, half:]
    chunk_id = lax.broadcasted_iota(jnp.int32, (rows, half), 0) % n_chunk
    step = 1
    k = 0
    while step < n_chunk:
        pr = jnp.where(chunk_id >= step, pltpu.roll(er, step, 0), 0.0)
        pi = jnp.where(chunk_id >= step, pltpu.roll(ei, step, 0), 0.0)
        pr, pi = _cmul(pr, pi, mu_ref[2 * k:2 * k + 1, :], mu_ref[2 * k + 1:2 * k + 2, :])
        er, ei = er + pr, ei + pi
        step *= 2
        k += 1
    sr = jnp.where(chunk_id >= 1, pltpu.roll(er, 1, 0), 0.0)
    si = jnp.where(chunk_id >= 1, pltpu.roll(ei, 1, 0), 0.0)
    state = jnp.concatenate([sr, si], axis=-1).astype(BF16)
    y = _dot(xb, mloc_ref[...]) + _dot(state, wc_ref[...])
    for j, (s, q) in enumerate(pieces):
        o_ref[q, pl.ds(s, rows, stride=SSM_T), :] = y[:, j * V7X_LANES:(j + 1) * V7X_LANES]


def _ssm(us, mats, bsz, seq):
    n_seq = min(SSM_SEQS, bsz)
    blk = pl.BlockSpec((SSM_SLABS, n_seq * seq, V7X_LANES), lambda i: (0, i, 0))
    return pl.pallas_call(
        functools.partial(_ssm_kernel, seq // SSM_T),
        grid=(bsz // n_seq,),
        in_specs=[blk] + [_resident(m.shape) for m in mats],
        out_specs=blk,
        out_shape=jax.ShapeDtypeStruct(us.shape, F32),
        compiler_params=_params(1),
        name="s5_scan",
    )(us, *mats)


def _ssm_matrices(a_re, a_im, log_dt, b_re, b_im, c_re, c_im, n_chunk):
    hi = lax.Precision.HIGHEST
    dt = jnp.exp(log_dt.astype(F32))[:, None]
    a_re = a_re.astype(F32)
    a_im = a_im.astype(F32)
    mag = jnp.exp(dt * a_re)
    ang = dt * a_im
    ab_re = mag * jnp.cos(ang)
    ab_im = mag * jnp.sin(ang)
    den = a_re * a_re + a_im * a_im
    nr = ab_re - 1.0
    ni = ab_im
    coef_re = (nr * a_re + ni * a_im) / den
    coef_im = (ni * a_re - nr * a_im) / den
    b_re = b_re.astype(F32)
    b_im = b_im.astype(F32)
    bb_re = coef_re[..., None] * b_re - coef_im[..., None] * b_im
    bb_im = coef_re[..., None] * b_im + coef_im[..., None] * b_re
    c_re = c_re.astype(F32)
    c_im = c_im.astype(F32)
    pw_r, pw_i = [jnp.ones_like(ab_re)], [jnp.zeros_like(ab_im)]
    for _ in range(SSM_T):
        r, i = _cmul(pw_r[-1], pw_i[-1], ab_re, ab_im)
        pw_r.append(r)
        pw_i.append(i)
    pw_r, pw_i = jnp.stack(pw_r), jnp.stack(pw_i)
    half = SSM_GROUPS * SSM_STATE
    width = SSM_T * SSM_WIDTH

    def spread(m):
        tiled = jnp.tile(m, (1, 1, SSM_GROUPS))
        row_g = jnp.arange(m.shape[1]) // (m.shape[1] // SSM_GROUPS)
        col_g = jnp.arange(tiled.shape[2]) // m.shape[2]
        return jnp.where(row_g[:, None] == col_g[None, :], tiled, 0.0)

    cp_r = c_re[None] * pw_r[:, :, None, :] - c_im[None] * pw_i[:, :, None, :]
    cp_i = c_re[None] * pw_i[:, :, None, :] + c_im[None] * pw_r[:, :, None, :]
    kern = (jnp.einsum('tghp,gpk->tgkh', cp_r[:SSM_T], bb_re, precision=hi)
            - jnp.einsum('tghp,gpk->tgkh', cp_i[:SSM_T], bb_im, precision=hi))
    lag_blocks = spread(kern.reshape(SSM_T, SSM_WIDTH, SSM_GROUP))
    no_block = jnp.zeros((SSM_WIDTH, SSM_WIDTH), F32)
    mloc = jnp.block([[lag_blocks[t - s] if t >= s else no_block for t in range(SSM_T)]
                      for s in range(SSM_T)])

    rev_r, rev_i = pw_r[SSM_T - 1::-1], pw_i[SSM_T - 1::-1]
    pb_r = rev_r[..., None] * bb_re[None] - rev_i[..., None] * bb_im[None]
    pb_i = rev_r[..., None] * bb_im[None] + rev_i[..., None] * bb_re[None]

    def drive(m):
        m = jnp.transpose(m, (0, 1, 3, 2)).reshape(SSM_T, SSM_WIDTH, SSM_STATE)
        return spread(m).reshape(width, half)

    we = jnp.concatenate([drive(pb_r), drive(pb_i)], axis=1)

    def read(m):
        m = jnp.transpose(m[1:], (0, 1, 3, 2)).reshape(SSM_T, half, SSM_GROUP)
        return jnp.concatenate(list(spread(m)), axis=1)

    wc = jnp.concatenate([read(cp_r), -read(cp_i)], axis=0)

    pr, pi = pw_r[SSM_T].reshape(1, half), pw_i[SSM_T].reshape(1, half)
    mus = []
    step = 1
    while step < n_chunk:
        mus += [pr, pi]
        pr, pi = _cmul(pr, pi, pr, pi)
        step *= 2
    return mloc.astype(BF16), we.astype(BF16), wc.astype(BF16), jnp.concatenate(mus, axis=0)


def _ab_out_tile(x_ref, ya_ref, ys_ref, us_ref, d_ref, wglu_ref, bglu_ref, wa_ref, wb_ref, g_ref, b_ref):
    ys = jnp.concatenate([ys_ref[q] for q in range(SSM_SLABS)], axis=-1)
    us = jnp.concatenate([us_ref[q] for q in range(SSM_SLABS)], axis=-1)
    y = ys + d_ref[...] * us
    gl = _gelu_tanh(y)
    yb = gl * jax.nn.sigmoid(_dot(gl.astype(BF16), wglu_ref[...]) + bglu_ref[...])
    mix = _dot(ya_ref[...], wa_ref[...]) + _dot(yb.astype(BF16), wb_ref[...])
    return _layer_norm(ALPHA * x_ref[...] + mix, g_ref[...], b_ref[...])


def _ab_out_operands(x, ya, ys, us, d, wglu, bglu, wa, wb, g, b):
    n_tiles = x.shape[0] // ROW_TILE
    consts = (d, wglu, bglu, wa, wb, g, b)
    specs = [_tile_rows(D_MODEL, n_tiles), _tile_rows(ATT_WIDTH, n_tiles), _tile_slabs(n_tiles),
             _tile_slabs(n_tiles)] + [_resident(c.shape) for c in consts]
    return (x, ya, ys, us) + consts, specs


def _cd_tile(seq, x_ref, halo_ref, win_ref, pw_ref, ps_ref, sg_ref, sb_ref, ws_ref, bs_ref,
             wout_ref, g_ref, b_ref):
    x = x_ref[...]
    h = _dot(x.astype(BF16), win_ref[...])
    xc = h[:, :POOL_WIDTH]
    row0 = (pl.program_id(0) * ROW_TILE) % seq
    hc = _dot(halo_ref[...].astype(BF16), win_ref[:, :POOL_WIDTH])
    hc = jnp.where(row0 == 0, 0.0, hc)
    ext = jnp.concatenate([hc, xc], axis=0)
    pos = row0 + lax.broadcasted_iota(jnp.int32, (ROW_TILE, POOL_GROUP), 0)
    acc = ext
    ys = []
    for gi, w in enumerate(POOL_WINDOWS):
        acc = acc + pltpu.roll(acc, w // 2, 0)
        cnt = jnp.minimum(pos + 1, w).astype(F32)
        pooled = acc[POOL_HALO:, :POOL_GROUP] / cnt - xc[:, gi * POOL_GROUP:(gi + 1) * POOL_GROUP]
        ys.append(_dot(pooled.astype(BF16), pw_ref[gi]))
        if gi + 1 < len(POOL_WINDOWS):
            acc = acc[:, POOL_GROUP:]
    yc = jnp.concatenate(ys, axis=-1) * ps_ref[...]

    z = _gelu_tanh(h[:, POOL_WIDTH:])
    u = z[:, :SGU_WIDTH]
    v = _layer_norm(z[:, SGU_WIDTH:], sg_ref[...], sb_ref[...]).astype(BF16)
    hd = SGU_WIDTH // SGU_HEADS
    blocks = []
    for r in range(ROW_TILE // SGU_CHUNK):
        rs = slice(r * SGU_CHUNK, (r + 1) * SGU_CHUNK)
        heads = [_dot(ws_ref[hh], v[rs, hh * hd:(hh + 1) * hd]) for hh in range(SGU_HEADS)]
        blocks.append(jnp.concatenate(heads, axis=-1) + bs_ref[...])
    yd = u * jnp.concatenate(blocks, axis=0)

    mix = _dot(yc.astype(BF16), wout_ref[:POOL_WIDTH, :]) + _dot(yd.astype(BF16), wout_ref[POOL_WIDTH:, :])
    return _layer_norm(ALPHA * x + mix, g_ref[...], b_ref[...])


def _cd_operands(x, win, pw, ps, sg, sb, ws, bs, wout, g, b):
    n_tiles = x.shape[0] // ROW_TILE
    halo_per_tile = ROW_TILE // POOL_HALO
    halo = pl.BlockSpec((POOL_HALO, D_MODEL),
                        lambda i: (jnp.maximum(jnp.minimum(i, n_tiles - 1) * halo_per_tile - 1, 0), 0))
    consts = (win, pw, ps, sg, sb, ws, bs, wout, g, b)
    return (x, x) + consts, [_tile_rows(D_MODEL, n_tiles), halo] + [_resident(c.shape) for c in consts]


def kernel(x, ln_g, ln_b, ffn_w_gate, ffn_w_up, ffn_w_down, ab_w_in, ab_w_out, att_rel_bias, ssm_a_re, ssm_a_im, ssm_log_dt, ssm_b_re, ssm_b_im, ssm_c_re, ssm_c_im, ssm_d, ssm_w_glu, ssm_b_glu, cd_w_in, cd_w_out, pool_w, pool_scale, sgu_ln_g, sgu_ln_b, sgu_w_s, sgu_b_s):
    bsz, seq, d_model = x.shape
    assert d_model == D_MODEL and seq % ROW_TILE == 0 and seq % Q_TILE == 0 and seq % SSM_T == 0
    assert bsz % min(SSM_SEQS, bsz) == 0 and ln_g.shape[0] == DEPTH
    n = bsz * seq
    h = x.reshape(n, D_MODEL).astype(F32)

    def ln_params(l, j):
        return ln_g[l, j].astype(F32).reshape(1, D_MODEL), ln_b[l, j].astype(F32).reshape(1, D_MODEL)

    wg_all, wu_all, wd_all = ffn_w_gate.astype(BF16), ffn_w_up.astype(BF16), ffn_w_down.astype(BF16)

    def ffn(pre_fn, operands, l, j, name):
        pre_args, pre_specs = operands
        return _ffn_ln(n, pre_fn, pre_args, pre_specs, wg_all, wu_all, wd_all, l, j,
                       *ln_params(l, j if j == 0 else 2), name)

    q_scale = jnp.concatenate([jnp.full((ATT_WIDTH,), ATT_HEAD_DIM ** -0.5 * LOG2_E, F32),
                               jnp.ones((2 * ATT_WIDTH + SSM_WIDTH,), F32)])
    tril = jnp.tril(jnp.ones((SGU_CHUNK, SGU_CHUNK), F32))

    for l in range(DEPTH):
        h = ffn(_load_tile, ((h,), [_tile_rows(D_MODEL, n // ROW_TILE)]), l, 0, "ffn_ln")
        i = l // 2
        if l % 2 == 0:
            qkv, us = _ab_in(h, (ab_w_in[i].astype(F32) * q_scale).astype(BF16))
            ya = _attention(qkv, _attention_bias(att_rel_bias[i]), bsz, seq)
            mats = _ssm_matrices(ssm_a_re[i], ssm_a_im[i], ssm_log_dt[i], ssm_b_re[i], ssm_b_im[i],
                                 ssm_c_re[i], ssm_c_im[i], seq // SSM_T)
            ys = _ssm(us, mats, bsz, seq)
            w_out = ab_w_out[i].astype(BF16)
            operands = _ab_out_operands(
                h, ya, ys, us, ssm_d[i].astype(F32).reshape(1, SSM_WIDTH), ssm_w_glu[i].astype(BF16),
                ssm_b_glu[i].astype(F32).reshape(1, SSM_WIDTH), w_out[:ATT_WIDTH], w_out[ATT_WIDTH:],
                *ln_params(l, 1))
            h = ffn(_ab_out_tile, operands, l, 1, "ab_out_ffn_ln")
        else:
            bs_tbl = jnp.repeat(sgu_b_s[i].astype(F32).T, SGU_WIDTH // SGU_HEADS, axis=1)
            operands = _cd_operands(
                h, cd_w_in[i].astype(BF16), pool_w[i].astype(BF16), pool_scale[i].astype(F32).reshape(1, POOL_WIDTH),
                sgu_ln_g[i].astype(F32).reshape(1, SGU_WIDTH), sgu_ln_b[i].astype(F32).reshape(1, SGU_WIDTH),
                (sgu_w_s[i].astype(F32) * tril).astype(BF16), bs_tbl, cd_w_out[i].astype(BF16), *ln_params(l, 1))
            h = ffn(functools.partial(_cd_tile, seq), operands, l, 1, "cd_ffn_ln")
    return h.reshape(bsz, seq, D_MODEL).astype(x.dtype)
```

```python
import functools

import jax
import jax.numpy as jnp
from jax import lax
from jax.experimental import pallas as pl
from jax.experimental.pallas import tpu as pltpu

BF16 = jnp.bfloat16
F32 = jnp.float32

DEPTH = 4
D_MODEL = 1024
D_FF = 2816
ALPHA = (2 * DEPTH) ** 0.25
LN_EPS = 1e-5
NEG_INF = -1e30
LOG2_E = 1.4426950408889634

CHUNK = 64
N_PREV_CHUNKS = 8
REL_CLIP = 128
ATT_HEAD_DIM = 64
ATT_WIDTH = 768
ATT_HEADS = 12
SSM_WIDTH = 256
SSM_GROUP = 16
SSM_GROUPS = 16
SSM_STATE = 64
POOL_WINDOWS = (2, 4, 8, 16)
POOL_WIDTH = 512
POOL_GROUP = 128
SGU_WIDTH = 512
SGU_CHUNK = 128
SGU_HEADS = 4

V7X_LANES = 128
V7X_MXU_DIM = 256
V7X_VMEM_LIMIT_BYTES = 60000 * 1024

ROW_TILE = 512
FF_CHUNK = V7X_MXU_DIM
LN_BLOCKS = 8
ANCHOR_ROWS = 16
Q_TILE = 4 * CHUNK
K_WINDOW = Q_TILE + N_PREV_CHUNKS * CHUNK
SSM_T = 8
SSM_SEQS = 2
SSM_SLABS = SSM_WIDTH // V7X_LANES
POOL_HALO = 16


def _params(n_axes):
    return pltpu.CompilerParams(
        dimension_semantics=("arbitrary",) * n_axes,
        vmem_limit_bytes=V7X_VMEM_LIMIT_BYTES)


def _resident(shape):
    nd = len(shape)
    return pl.BlockSpec(shape, lambda *_: (0,) * nd, pipeline_mode=pl.Buffered(1))


def _layer_norm(z, g, b):
    mu = jnp.mean(z, axis=-1, keepdims=True)
    zc = z - mu
    var = jnp.mean(zc * zc, axis=-1, keepdims=True)
    return zc * lax.rsqrt(var + LN_EPS) * g + b


def _gelu_tanh(x):
    return 0.5 * x * (1.0 + jnp.tanh(0.7978845608028654 * (x + 0.044715 * (x * x * x))))


def _dot(a, b):
    return jnp.dot(a, b, preferred_element_type=F32)


def _ffn_kernel(n_tiles, n_pre, pre_fn, *refs):
    pre_refs = refs[:n_pre]
    wg_ref, wu_ref, wd_ref, g_ref, b_ref, o_ref, h_scr, x_scr, y_scr, xb_scr = refs[n_pre:]
    i = pl.program_id(0)
    ln_rows = ROW_TILE // LN_BLOCKS

    def finish_previous(rows):
        out = _layer_norm(ALPHA * x_scr[rows, :] + 0.5 * y_scr[rows, :], g_ref[...], b_ref[...])
        o_ref[rows, :] = out
        return out

    def branch_into_scratch(finish):
        x = pre_fn(*pre_refs)
        xb_scr[...] = x.astype(BF16)
        for c in range(D_FF // FF_CHUNK):
            sl = slice(c * FF_CHUNK, (c + 1) * FF_CHUNK)
            xb = xb_scr[...]
            gate = _dot(xb, wg_ref[:, sl])
            up = _dot(xb, wu_ref[:, sl])
            h_scr[:, sl] = (gate * jax.nn.sigmoid(gate) * up).astype(BF16)
            if finish and c < LN_BLOCKS:
                out = finish_previous(slice(c * ln_rows, (c + 1) * ln_rows))
                bits = lax.bitcast_convert_type(out[:ANCHOR_ROWS, :], jnp.uint32)
                zero = lax.shift_right_logical(lax.shift_right_logical(bits, jnp.uint32(16)), jnp.uint32(16))
                xb_scr[:ANCHOR_ROWS, :] = xb_scr[:ANCHOR_ROWS, :] + zero.astype(F32).astype(BF16)
        y = _dot(h_scr[...], wd_ref[...])
        x_scr[...] = x
        y_scr[...] = y

    @pl.when(i == 0)
    def _():
        branch_into_scratch(False)

    @pl.when(jnp.logical_and(i > 0, i < n_tiles))
    def _():
        branch_into_scratch(True)

    @pl.when(i == n_tiles)
    def _():
        finish_previous(slice(None))


def _tile_rows(width, n_tiles):
    return pl.BlockSpec((ROW_TILE, width), lambda i: (jnp.minimum(i, n_tiles - 1), 0))


def _tile_slabs(n_tiles):
    return pl.BlockSpec((SSM_SLABS, ROW_TILE, V7X_LANES), lambda i: (0, jnp.minimum(i, n_tiles - 1), 0))


def _load_tile(x_ref):
    return x_ref[...]


def _ffn_ln(n, pre_fn, pre_args, pre_specs, wg, wu, wd, layer, pos, g, b, name):
    n_tiles = n // ROW_TILE

    def picked(rows, cols):
        return pl.BlockSpec((None, None, rows, cols), lambda i: (layer, pos, 0, 0), pipeline_mode=pl.Buffered(1))

    return pl.pallas_call(
        functools.partial(_ffn_kernel, n_tiles, len(pre_args), pre_fn),
        grid=(n_tiles + 1,),
        in_specs=list(pre_specs) + [picked(D_MODEL, D_FF), picked(D_MODEL, D_FF), picked(D_FF, D_MODEL),
                                    _resident((1, D_MODEL)), _resident((1, D_MODEL))],
        out_specs=pl.BlockSpec((ROW_TILE, D_MODEL), lambda i: (jnp.maximum(i - 1, 0), 0)),
        out_shape=jax.ShapeDtypeStruct((n, D_MODEL), F32),
        scratch_shapes=[pltpu.VMEM((ROW_TILE, D_FF), BF16), pltpu.VMEM((ROW_TILE, D_MODEL), F32),
                        pltpu.VMEM((ROW_TILE, D_MODEL), F32), pltpu.VMEM((ROW_TILE, D_MODEL), BF16)],
        compiler_params=_params(1),
        name=name,
    )(*pre_args, wg, wu, wd, g, b)


def _ab_in_kernel(x_ref, w_ref, qkv_ref, us_ref):
    h = _dot(x_ref[...].astype(BF16), w_ref[...])
    qkv_ref[...] = h[:, :3 * ATT_WIDTH].astype(BF16)
    for q in range(SSM_SLABS):
        lo = 3 * ATT_WIDTH + q * V7X_LANES
        us_ref[q] = h[:, lo:lo + V7X_LANES]


def _ab_in(x, w):
    n = x.shape[0]
    width = 3 * ATT_WIDTH + SSM_WIDTH
    return pl.pallas_call(
        _ab_in_kernel,
        grid=(n // ROW_TILE,),
        in_specs=[pl.BlockSpec((ROW_TILE, D_MODEL), lambda i: (i, 0)), _resident((D_MODEL, width))],
        out_specs=[pl.BlockSpec((ROW_TILE, 3 * ATT_WIDTH), lambda i: (i, 0)),
                   pl.BlockSpec((SSM_SLABS, ROW_TILE, V7X_LANES), lambda i: (0, i, 0))],
        out_shape=[jax.ShapeDtypeStruct((n, 3 * ATT_WIDTH), BF16),
                   jax.ShapeDtypeStruct((SSM_SLABS, n, V7X_LANES), F32)],
        compiler_params=_params(1),
        name="ab_in",
    )(x, w)


def _attn_kernel(q_ref, k_ref, v_ref, bias_ref, o_ref):
    seq = q_ref.shape[0]
    lane = lax.broadcasted_iota(jnp.int32, (Q_TILE, V7X_LANES), 1)
    first_head = lane < ATT_HEAD_DIM
    for t in range(seq // Q_TILE):
        q0 = t * Q_TILE
        k0 = max(0, q0 + Q_TILE - K_WINDOW)
        width = q0 + Q_TILE - k0
        q = q_ref[q0:q0 + Q_TILE, :]
        k = k_ref[k0:k0 + width, :]
        v = v_ref[k0:k0 + width, :]
        zero = jnp.zeros_like(q)
        qs = jnp.concatenate([jnp.where(first_head, q, zero), jnp.where(first_head, zero, q)], axis=0)
        s = lax.dot_general(qs, k, (((1,), (1,)), ((), ())), preferred_element_type=F32)
        s = s + bias_ref[:, :, K_WINDOW - width:].reshape(2 * Q_TILE, width)
        p = jnp.exp2(s - jnp.max(s, axis=-1, keepdims=True))
        o = _dot(p.astype(BF16), v) / jnp.sum(p, axis=-1, keepdims=True)
        o_ref[q0:q0 + Q_TILE, :] = jnp.where(first_head, o[:Q_TILE], o[Q_TILE:]).astype(BF16)


def _attention(qkv, bias, bsz, seq):
    pairs = ATT_WIDTH // V7X_LANES
    qkv3 = qkv.reshape(bsz, seq, 3 * ATT_WIDTH)

    def col_block(offset):
        return pl.BlockSpec((None, seq, V7X_LANES), lambda b, j: (b, 0, offset + j))

    out = pl.pallas_call(
        _attn_kernel,
        grid=(bsz, pairs),
        in_specs=[col_block(0), col_block(pairs), col_block(2 * pairs),
                  pl.BlockSpec((2, Q_TILE, K_WINDOW), lambda b, j: (j, 0, 0))],
        out_specs=pl.BlockSpec((None, seq, V7X_LANES), lambda b, j: (b, 0, j)),
        out_shape=jax.ShapeDtypeStruct((bsz, seq, ATT_WIDTH), BF16),
        compiler_params=_params(2),
        name="band_attention",
    )(qkv3, qkv3, qkv3, bias)
    return out.reshape(bsz * seq, ATT_WIDTH)


def _bias_kernel(rev_ref, o_ref):
    span = rev_ref.shape[-1]
    skew = pltpu.roll(jnp.broadcast_to(rev_ref[0], (Q_TILE, span)), 1, 1, stride=1, stride_axis=0)
    table = skew[:, span - K_WINDOW:]
    qc = lax.broadcasted_iota(jnp.int32, table.shape, 0) // CHUNK
    kc = lax.broadcasted_iota(jnp.int32, table.shape, 1) // CHUNK
    in_band = jnp.logical_and(kc >= qc, kc <= qc + N_PREV_CHUNKS)
    o_ref[0] = jnp.where(in_band, table, NEG_INF)


def _attention_bias(rel_bias):
    heads = rel_bias.shape[0]
    rb = rel_bias.astype(F32) * LOG2_E
    d_min = N_PREV_CHUNKS * CHUNK - (K_WINDOW - 1)
    d_max = N_PREV_CHUNKS * CHUNK + Q_TILE - 1
    span = d_max - d_min + 2
    assert span == Q_TILE + K_WINDOW and span % V7X_LANES == 0
    by_dist = jnp.concatenate([jnp.repeat(rb[:, :1], -REL_CLIP - d_min, axis=1), rb,
                               jnp.repeat(rb[:, -1:], d_max - REL_CLIP, axis=1)], axis=1)
    rev = jnp.concatenate([by_dist[:, ::-1], jnp.zeros((heads, 1), F32)], axis=1).reshape(heads, 1, span)
    return pl.pallas_call(
        _bias_kernel,
        grid=(heads,),
        in_specs=[pl.BlockSpec((1, 1, span), lambda h: (h, 0, 0))],
        out_specs=pl.BlockSpec((1, Q_TILE, K_WINDOW), lambda h: (h, 0, 0)),
        out_shape=jax.ShapeDtypeStruct((heads, Q_TILE, K_WINDOW), F32),
        compiler_params=_params(1),
        name="bias_table",
    )(rev)


def _cmul(ar, ai, br, bi):
    return ar * br - ai * bi, ar * bi + ai * br


def _ssm_kernel(n_chunk, u_ref, mloc_ref, we_ref, wc_ref, mu_ref, o_ref):
    rows = u_ref.shape[1] // SSM_T
    half = SSM_GROUPS * SSM_STATE
    pieces = [(s, q) for s in range(SSM_T) for q in range(SSM_SLABS)]
    xb = jnp.concatenate([u_ref[q, pl.ds(s, rows, stride=SSM_T), :] for s, q in pieces], axis=-1).astype(BF16)
    e = _dot(xb, we_ref[...])
    er, ei = e[:, :half], e[:, half:]
    chunk_id = lax.broadcasted_iota(jnp.int32, (rows, half), 0) % n_chunk
    step = 1
    k = 0
    while step < n_chunk:
        pr = jnp.where(chunk_id >= step, pltpu.roll(er, step, 0), 0.0)
        pi = jnp.where(chunk_id >= step, pltpu.roll(ei, step, 0), 0.0)
        pr, pi = _cmul(pr, pi, mu_ref[2 * k:2 * k + 1, :], mu_ref[2 * k + 1:2 * k + 2, :])
        er, ei = er + pr, ei + pi
        step *= 2
        k += 1
    sr = jnp.where(chunk_id >= 1, pltpu.roll(er, 1, 0), 0.0)
    si = jnp.where(chunk_id >= 1, pltpu.roll(ei, 1, 0), 0.0)
    state = jnp.concatenate([sr, si], axis=-1).astype(BF16)
    y = _dot(xb, mloc_ref[...]) + _dot(state, wc_ref[...])
    for j, (s, q) in enumerate(pieces):
        o_ref[q, pl.ds(s, rows, stride=SSM_T), :] = y[:, j * V7X_LANES:(j + 1) * V7X_LANES]


def _ssm(us, mats, bsz, seq):
    n_seq = min(SSM_SEQS, bsz)
    blk = pl.BlockSpec((SSM_SLABS, n_seq * seq, V7X_LANES), lambda i: (0, i, 0))
    return pl.pallas_call(
        functools.partial(_ssm_kernel, seq // SSM_T),
        grid=(bsz // n_seq,),
        in_specs=[blk] + [_resident(m.shape) for m in mats],
        out_specs=blk,
        out_shape=jax.ShapeDtypeStruct(us.shape, F32),
        compiler_params=_params(1),
        name="s5_scan",
    )(us, *mats)


def _ssm_matrices(a_re, a_im, log_dt, b_re, b_im, c_re, c_im, n_chunk):
    hi = lax.Precision.HIGHEST
    dt = jnp.exp(log_dt.astype(F32))[:, None]
    a_re = a_re.astype(F32)
    a_im = a_im.astype(F32)
    mag = jnp.exp(dt * a_re)
    ang = dt * a_im
    ab_re = mag * jnp.cos(ang)
    ab_im = mag * jnp.sin(ang)
    den = a_re * a_re + a_im * a_im
    nr = ab_re - 1.0
    ni = ab_im
    coef_re = (nr * a_re + ni * a_im) / den
    coef_im = (ni * a_re - nr * a_im) / den
    b_re = b_re.astype(F32)
    b_im = b_im.astype(F32)
    bb_re = coef_re[..., None] * b_re - coef_im[..., None] * b_im
    bb_im = coef_re[..., None] * b_im + coef_im[..., None] * b_re
    c_re = c_re.astype(F32)
    c_im = c_im.astype(F32)
    pw_r, pw_i = [jnp.ones_like(ab_re)], [jnp.zeros_like(ab_im)]
    for _ in range(SSM_T):
        r, i = _cmul(pw_r[-1], pw_i[-1], ab_re, ab_im)
        pw_r.append(r)
        pw_i.append(i)
    pw_r, pw_i = jnp.stack(pw_r), jnp.stack(pw_i)
    half = SSM_GROUPS * SSM_STATE
    width = SSM_T * SSM_WIDTH

    def spread(m):
        tiled = jnp.tile(m, (1, 1, SSM_GROUPS))
        row_g = jnp.arange(m.shape[1]) // (m.shape[1] // SSM_GROUPS)
        col_g = jnp.arange(tiled.shape[2]) // m.shape[2]
        return jnp.where(row_g[:, None] == col_g[None, :], tiled, 0.0)

    cp_r = c_re[None] * pw_r[:, :, None, :] - c_im[None] * pw_i[:, :, None, :]
    cp_i = c_re[None] * pw_i[:, :, None, :] + c_im[None] * pw_r[:, :, None, :]
    kern = (jnp.einsum('tghp,gpk->tgkh', cp_r[:SSM_T], bb_re, precision=hi)
            - jnp.einsum('tghp,gpk->tgkh', cp_i[:SSM_T], bb_im, precision=hi))
    lag_blocks = spread(kern.reshape(SSM_T, SSM_WIDTH, SSM_GROUP))
    no_block = jnp.zeros((SSM_WIDTH, SSM_WIDTH), F32)
    mloc = jnp.block([[lag_blocks[t - s] if t >= s else no_block for t in range(SSM_T)]
                      for s in range(SSM_T)])

    rev_r, rev_i = pw_r[SSM_T - 1::-1], pw_i[SSM_T - 1::-1]
    pb_r = rev_r[..., None] * bb_re[None] - rev_i[..., None] * bb_im[None]
    pb_i = rev_r[..., None] * bb_im[None] + rev_i[..., None] * bb_re[None]

    def drive(m):
        m = jnp.transpose(m, (0, 1, 3, 2)).reshape(SSM_T, SSM_WIDTH, SSM_STATE)
        return spread(m).reshape(width, half)

    we = jnp.concatenate([drive(pb_r), drive(pb_i)], axis=1)

    def read(m):
        m = jnp.transpose(m[1:], (0, 1, 3, 2)).reshape(SSM_T, half, SSM_GROUP)
        return jnp.concatenate(list(spread(m)), axis=1)

    wc = jnp.concatenate([read(cp_r), -read(cp_i)], axis=0)

    pr, pi = pw_r[SSM_T].reshape(1, half), pw_i[SSM_T].reshape(1, half)
    mus = []
    step = 1
    while step < n_chunk:
        mus += [pr, pi]
        pr, pi = _cmul(pr, pi, pr, pi)
        step *= 2
    return mloc.astype(BF16), we.astype(BF16), wc.astype(BF16), jnp.concatenate(mus, axis=0)


def _ab_out_tile(x_ref, ya_ref, ys_ref, us_ref, d_ref, wglu_ref, bglu_ref, wa_ref, wb_ref, g_ref, b_ref):
    ys = jnp.concatenate([ys_ref[q] for q in range(SSM_SLABS)], axis=-1)
    us = jnp.concatenate([us_ref[q] for q in range(SSM_SLABS)], axis=-1)
    y = ys + d_ref[...] * us
    gl = _gelu_tanh(y)
    yb = gl * jax.nn.sigmoid(_dot(gl.astype(BF16), wglu_ref[...]) + bglu_ref[...])
    mix = _dot(ya_ref[...], wa_ref[...]) + _dot(yb.astype(BF16), wb_ref[...])
    return _layer_norm(ALPHA * x_ref[...] + mix, g_ref[...], b_ref[...])


def _ab_out_operands(x, ya, ys, us, d, wglu, bglu, wa, wb, g, b):
    n_tiles = x.shape[0] // ROW_TILE
    consts = (d, wglu, bglu, wa, wb, g, b)
    specs = [_tile_rows(D_MODEL, n_tiles), _tile_rows(ATT_WIDTH, n_tiles), _tile_slabs(n_tiles),
             _tile_slabs(n_tiles)] + [_resident(c.shape) for c in consts]
    return (x, ya, ys, us) + consts, specs


def _cd_tile(seq, x_ref, halo_ref, win_ref, pw_ref, ps_ref, sg_ref, sb_ref, ws_ref, bs_ref,
             wout_ref, g_ref, b_ref):
    x = x_ref[...]
    h = _dot(x.astype(BF16), win_ref[...])
    xc = h[:, :POOL_WIDTH]
    row0 = (pl.program_id(0) * ROW_TILE) % seq
    hc = _dot(halo_ref[...].astype(BF16), win_ref[:, :POOL_WIDTH])
    hc = jnp.where(row0 == 0, 0.0, hc)
    ext = jnp.concatenate([hc, xc], axis=0)
    pos = row0 + lax.broadcasted_iota(jnp.int32, (ROW_TILE, POOL_GROUP), 0)
    acc = ext
    ys = []
    for gi, w in enumerate(POOL_WINDOWS):
        acc = acc + pltpu.roll(acc, w // 2, 0)
        cnt = jnp.minimum(pos + 1, w).astype(F32)
        pooled = acc[POOL_HALO:, :POOL_GROUP] / cnt - xc[:, gi * POOL_GROUP:(gi + 1) * POOL_GROUP]
        ys.append(_dot(pooled.astype(BF16), pw_ref[gi]))
        if gi + 1 < len(POOL_WINDOWS):
            acc = acc[:, POOL_GROUP:]
    yc = jnp.concatenate(ys, axis=-1) * ps_ref[...]

    z = _gelu_tanh(h[:, POOL_WIDTH:])
    u = z[:, :SGU_WIDTH]
    v = _layer_norm(z[:, SGU_WIDTH:], sg_ref[...], sb_ref[...]).astype(BF16)
    hd = SGU_WIDTH // SGU_HEADS
    blocks = []
    for r in range(ROW_TILE // SGU_CHUNK):
        rs = slice(r * SGU_CHUNK, (r + 1) * SGU_CHUNK)
        heads = [_dot(ws_ref[hh], v[rs, hh * hd:(hh + 1) * hd]) for hh in range(SGU_HEADS)]
        blocks.append(jnp.concatenate(heads, axis=-1) + bs_ref[...])
    yd = u * jnp.concatenate(blocks, axis=0)

    mix = _dot(yc.astype(BF16), wout_ref[:POOL_WIDTH, :]) + _dot(yd.astype(BF16), wout_ref[POOL_WIDTH:, :])
    return _layer_norm(ALPHA * x + mix, g_ref[...], b_ref[...])


def _cd_operands(x, win, pw, ps, sg, sb, ws, bs, wout, g, b):
    n_tiles = x.shape[0] // ROW_TILE
    halo_per_tile = ROW_TILE // POOL_HALO
    halo = pl.BlockSpec((POOL_HALO, D_MODEL),
                        lambda i: (jnp.maximum(jnp.minimum(i, n_tiles - 1) * halo_per_tile - 1, 0), 0))
    consts = (win, pw, ps, sg, sb, ws, bs, wout, g, b)
    return (x, x) + consts, [_tile_rows(D_MODEL, n_tiles), halo] + [_resident(c.shape) for c in consts]


def kernel(x, ln_g, ln_b, ffn_w_gate, ffn_w_up, ffn_w_down, ab_w_in, ab_w_out, att_rel_bias, ssm_a_re, ssm_a_im, ssm_log_dt, ssm_b_re, ssm_b_im, ssm_c_re, ssm_c_im, ssm_d, ssm_w_glu, ssm_b_glu, cd_w_in, cd_w_out, pool_w, pool_scale, sgu_ln_g, sgu_ln_b, sgu_w_s, sgu_b_s):
    bsz, seq, d_model = x.shape
    assert d_model == D_MODEL and seq % ROW_TILE == 0 and seq % Q_TILE == 0 and seq % SSM_T == 0
    assert bsz % min(SSM_SEQS, bsz) == 0 and ln_g.shape[0] == DEPTH
    n = bsz * seq
    h = x.reshape(n, D_MODEL).astype(F32)

    def ln_params(l, j):
        return ln_g[l, j].astype(F32).reshape(1, D_MODEL), ln_b[l, j].astype(F32).reshape(1, D_MODEL)

    wg_all, wu_all, wd_all = ffn_w_gate.astype(BF16), ffn_w_up.astype(BF16), ffn_w_down.astype(BF16)

    def ffn(pre_fn, operands, l, j, name):
        pre_args, pre_specs = operands
        return _ffn_ln(n, pre_fn, pre_args, pre_specs, wg_all, wu_all, wd_all, l, j,
                       *ln_params(l, j if j == 0 else 2), name)

    q_scale = jnp.concatenate([jnp.full((ATT_WIDTH,), ATT_HEAD_DIM ** -0.5 * LOG2_E, F32),
                               jnp.ones((2 * ATT_WIDTH + SSM_WIDTH,), F32)])
    tril = jnp.tril(jnp.ones((SGU_CHUNK, SGU_CHUNK), F32))

    for l in range(DEPTH):
        h = ffn(_load_tile, ((h,), [_tile_rows(D_MODEL, n // ROW_TILE)]), l, 0, "ffn_ln")
        i = l // 2
        if l % 2 == 0:
            qkv, us = _ab_in(h, (ab_w_in[i].astype(F32) * q_scale).astype(BF16))
            ya = _attention(qkv, _attention_bias(att_rel_bias[i]), bsz, seq)
            mats = _ssm_matrices(ssm_a_re[i], ssm_a_im[i], ssm_log_dt[i], ssm_b_re[i], ssm_b_im[i],
                                 ssm_c_re[i], ssm_c_im[i], seq // SSM_T)
            ys = _ssm(us, mats, bsz, seq)
            w_out = ab_w_out[i].astype(BF16)
            operands = _ab_out_operands(
                h, ya, ys, us, ssm_d[i].astype(F32).reshape(1, SSM_WIDTH), ssm_w_glu[i].astype(BF16),
                ssm_b_glu[i].astype(F32).reshape(1, SSM_WIDTH), w_out[:ATT_WIDTH], w_out[ATT_WIDTH:],
                *ln_params(l, 1))
            h = ffn(_ab_out_tile, operands, l, 1, "ab_out_ffn_ln")
        else:
            bs_tbl = jnp.repeat(sgu_b_s[i].astype(F32).T, SGU_WIDTH // SGU_HEADS, axis=1)
            operands = _cd_operands(
                h, cd_w_in[i].astype(BF16), pool_w[i].astype(BF16), pool_scale[i].astype(F32).reshape(1, POOL_WIDTH),
                sgu_ln_g[i].astype(F32).reshape(1, SGU_WIDTH), sgu_ln_b[i].astype(F32).reshape(1, SGU_WIDTH),
                (sgu_w_s[i].astype(F32) * tril).astype(BF16), bs_tbl, cd_w_out[i].astype(BF16), *ln_params(l, 1))
            h = ffn(functools.partial(_cd_tile, seq), operands, l, 1, "cd_ffn_ln")
    return h.reshape(bsz, seq, D_MODEL).astype(x.dtype)
```
